```python
import math
import jax, jax.numpy as jnp
from jax import lax
import numpy as np

D_MODEL = 1024
BATCH = 16
SEQ = 2048
DEPTH = 1

D_MIX = D_MODEL
SB_HEADS = 8
HEAD_DIM = 64
SB_WIDTH = SB_HEADS * HEAD_DIM
CONV_GROUPS = 8
CONV_WIDTH = D_MIX - SB_WIDTH
CONV_K = 3
SB_BLOCK = 128
N_KEYS = 128
N_EXPERTS = N_KEYS * N_KEYS
PEER_HEADS = 8
PEER_DK = 256
D_HALF = PEER_DK // 2
PEER_TOPK = 16
PEER_CHUNK = 128
EPS = 1e-6

kernel_name = "hymba_sb_conv_peer_block"


def rmsnorm(x, g):
    xf = x.astype(jnp.float32)
    y = xf * lax.rsqrt(jnp.mean(xf * xf, axis=-1, keepdims=True) + EPS)
    return (y * g.astype(jnp.float32)).astype(x.dtype)


def stick_breaking_attention(q, k, v):
    S = q.shape[2]
    scale = 1.0 / math.sqrt(q.shape[-1])
    outs = []
    for i in range(S // SB_BLOCK):
        t0 = i * SB_BLOCK
        L = t0 + SB_BLOCK
        qb = q[:, :, t0:L].astype(jnp.float32)
        kb = k[:, :, :L].astype(jnp.float32)
        vb = v[:, :, :L].astype(jnp.float32)
        z = jnp.einsum('bhqd,bhkd->bhqk', qb, kb) * scale
        t_pos = t0 + jnp.arange(SB_BLOCK)[:, None]
        s_pos = jnp.arange(L)[None, :]
        mask = s_pos < t_pos
        log_1m = jnp.where(mask, jax.nn.log_sigmoid(-z), 0.0)
        suffix = lax.cumsum(log_1m, axis=3, reverse=True) - log_1m
        a = jnp.where(mask, jnp.exp(jax.nn.log_sigmoid(z) + suffix), 0.0)
        outs.append(jnp.einsum('bhqk,bhkd->bhqd', a, vb))
    return jnp.concatenate(outs, axis=2).astype(q.dtype)


def short_gated_conv(gb, gc, hc, conv_w, conv_b):
    S = hc.shape[1]
    u = gc * hc
    up = jnp.pad(u, ((0, 0), (CONV_K - 1, 0), (0, 0)))
    y = sum(conv_w[j] * up[:, j:j + S] for j in range(CONV_K)) + conv_b
    return gb * y


def peer(xn, wq, keys, u_tab, v_tab):
    B, S, D = xn.shape
    xt = xn.reshape(-1, PEER_CHUNK, D)

    def chunk(xc):
        q = (xc @ wq).reshape(PEER_CHUNK, PEER_HEADS, PEER_DK).astype(jnp.float32)
        s1 = jnp.einsum('thd,hnd->thn', q[..., :D_HALF], keys[0].astype(jnp.float32))
        s2 = jnp.einsum('thd,hnd->thn', q[..., D_HALF:], keys[1].astype(jnp.float32))
        v1, i1 = lax.top_k(s1, PEER_TOPK)
        v2, i2 = lax.top_k(s2, PEER_TOPK)
        cand = (v1[..., :, None] + v2[..., None, :]).reshape(PEER_CHUNK, PEER_HEADS, PEER_TOPK * PEER_TOPK)
        cand_idx = (i1[..., :, None] * N_KEYS + i2[..., None, :]).reshape(PEER_CHUNK, PEER_HEADS, PEER_TOPK * PEER_TOPK)
        best, pos = lax.top_k(cand, PEER_TOPK)
        idx = jnp.take_along_axis(cand_idx, pos, axis=-1)
        g = jax.nn.softmax(best, axis=-1)
        u = u_tab[idx]
        act = jax.nn.gelu(jnp.einsum('td,thkd->thk', xc, u))
        w = g.astype(xc.dtype) * act
        return jnp.einsum('thk,thkd->td', w, v_tab[idx])

    return lax.map(chunk, xt).reshape(B, S, D)


def setup_inputs(seed: int = 0) -> dict:
    key = jax.random.key(seed)
    ks = jax.random.split(key, 14)
    f32 = jnp.float32
    nrm = lambda k, shape, s: jax.random.normal(k, shape, f32) * s
    gain = lambda k, shape: 1.0 + 0.02 * jax.random.normal(k, shape, f32)
    return {
        "x": jax.random.normal(ks[0], (BATCH, SEQ, D_MODEL), f32),
        "mix_norm_g": gain(ks[1], (DEPTH, D_MODEL)),
        "w_in": nrm(ks[2], (DEPTH, D_MODEL, 3 * SB_WIDTH + 3 * CONV_WIDTH), D_MODEL ** -0.5),
        "conv_w": nrm(ks[3], (DEPTH, CONV_K, CONV_WIDTH), 0.5),
        "conv_b": nrm(ks[4], (DEPTH, CONV_WIDTH), 0.02),
        "attn_out_g": gain(ks[5], (DEPTH, SB_WIDTH)),
        "conv_out_g": gain(ks[6], (DEPTH, CONV_WIDTH)),
        "w_out": nrm(ks[7], (DEPTH, D_MIX, D_MODEL), D_MIX ** -0.5),
        "ffn_norm_g": gain(ks[8], (DEPTH, D_MODEL)),
        "peer_wq": nrm(ks[9], (DEPTH, D_MODEL, PEER_HEADS * PEER_DK), D_MODEL ** -0.5),
        "peer_keys": nrm(ks[10], (DEPTH, 2, PEER_HEADS, N_KEYS, D_HALF), D_HALF ** -0.5),
        "peer_u": nrm(ks[11], (DEPTH, N_EXPERTS, D_MODEL), D_MODEL ** -0.5),
        "peer_v": nrm(ks[12], (DEPTH, N_EXPERTS, D_MODEL), 0.5),
        "final_norm_g": gain(ks[13], (D_MODEL,)),
    }


def reference(x, mix_norm_g, w_in, conv_w, conv_b, attn_out_g, conv_out_g, w_out,
              ffn_norm_g, peer_wq, peer_keys, peer_u, peer_v, final_norm_g):
    B, S, D = x.shape
    for layer in range(DEPTH):
        h = rmsnorm(x, mix_norm_g[layer])
        proj = h @ w_in[layer]
        q, k, v, gb, gc, hc = jnp.split(proj, 6, axis=-1)
        heads = lambda t: t.reshape(B, S, SB_HEADS, HEAD_DIM).transpose(0, 2, 1, 3)
        o_attn = stick_breaking_attention(heads(q), heads(k), heads(v))
        o_attn = o_attn.transpose(0, 2, 1, 3).reshape(B, S, SB_WIDTH)
        o_conv = short_gated_conv(gb, gc, hc, conv_w[layer], conv_b[layer])
        mixed = jnp.concatenate([rmsnorm(o_attn, attn_out_g[layer]),
                                 rmsnorm(o_conv, conv_out_g[layer])], axis=-1)
        x = x + mixed @ w_out[layer]
        xn = rmsnorm(x, ffn_norm_g[layer])
        x = x + peer(xn, peer_wq[layer], peer_keys[layer], peer_u[layer], peer_v[layer])
    return rmsnorm(x, final_norm_g)
```

```python
import functools
import math

import jax
import jax.numpy as jnp
from jax import lax
from jax.experimental import pallas as pl
from jax.experimental.pallas import tpu as pltpu

F32 = jnp.float32
BF16 = jnp.bfloat16

D_MODEL = 1024
SB_HEADS = 8
HEAD_DIM = 64
SB_WIDTH = SB_HEADS * HEAD_DIM
CONV_WIDTH = D_MODEL - SB_WIDTH
CONV_K = 3
N_KEYS = 128
PEER_HEADS = 8
PEER_DK = 256
D_HALF = PEER_DK // 2
PEER_TOPK = 16
N_PICKS = PEER_HEADS * PEER_TOPK
EPS = 1e-6

SUBLANES = 8
LANES = 128
ROW_TILES = D_MODEL // LANES

TM_PROJ = 512
TQ = 128
TM_MIX = 256
T_PEER = 8
TM_FINAL = 512
VMEM_LIMIT = 56 * 1024 * 1024


def _rms(x, g):
    return x * lax.rsqrt(jnp.mean(x * x, axis=-1, keepdims=True) + EPS) * g


def _inproj_kernel(x_ref, g_ref, w_ref, qkv_ref, cv_ref):
    h = _rms(x_ref[...], g_ref[...])
    p = jnp.dot(h.astype(BF16), w_ref[...], preferred_element_type=F32)
    qkv_ref[...] = p[:, : 3 * SB_WIDTH].astype(BF16)
    cv_ref[...] = p[:, 3 * SB_WIDTH:]


def _inproj(x2, g, w_bf16):
    m = x2.shape[0]
    n = w_bf16.shape[1]
    return pl.pallas_call(
        _inproj_kernel,
        grid=(m // TM_PROJ,),
        in_specs=[
            pl.BlockSpec((TM_PROJ, D_MODEL), lambda i: (i, 0)),
            pl.BlockSpec((1, D_MODEL), lambda i: (0, 0)),
            pl.BlockSpec((D_MODEL, n), lambda i: (0, 0)),
        ],
        out_specs=[
            pl.BlockSpec((TM_PROJ, 3 * SB_WIDTH), lambda i: (i, 0)),
            pl.BlockSpec((TM_PROJ, 3 * CONV_WIDTH), lambda i: (i, 0)),
        ],
        out_shape=[
            jax.ShapeDtypeStruct((m, 3 * SB_WIDTH), BF16),
            jax.ShapeDtypeStruct((m, 3 * CONV_WIDTH), F32),
        ],
        compiler_params=pltpu.CompilerParams(
            dimension_semantics=("parallel",), vmem_limit_bytes=VMEM_LIMIT),
        name="inproj",
    )(x2, g.reshape(1, D_MODEL), w_bf16)


def _softplus(z):
    return jnp.maximum(z, 0.0) + jnp.log1p(jnp.exp(-jnp.abs(z)))


def _attn_kernel(q_ref, k_ref, v_ref, o_ref):
    i = pl.program_id(2)
    scale = 1.0 / math.sqrt(HEAD_DIM)
    rows = lax.broadcasted_iota(jnp.int32, (TQ, TQ), 0)
    cols = lax.broadcasted_iota(jnp.int32, (TQ, TQ), 1)
    tri = (rows >= cols).astype(BF16)
    outs = []
    for hh in range(2):
        lo = hh * HEAD_DIM
        q = q_ref[:, lo:lo + HEAD_DIM]

        def body(jj, carry, q=q, lo=lo):
            o, r = carry
            j = i - jj
            s0 = pl.multiple_of(j * TQ, TQ)
            k = k_ref[pl.ds(s0, TQ), lo:lo + HEAD_DIM]
            v = v_ref[pl.ds(s0, TQ), lo:lo + HEAD_DIM]
            z = lax.dot_general(q, k, (((1,), (1,)), ((), ())), preferred_element_type=F32) * scale
            mask = (rows + jj * TQ) > cols
            l1m = jnp.where(mask, -_softplus(z), 0.0)
            l_hi = l1m.astype(BF16)
            l_lo = (l1m - l_hi.astype(F32)).astype(BF16)
            c = (jnp.dot(l_hi, tri, preferred_element_type=F32)
                 + jnp.dot(l_lo, tri, preferred_element_type=F32) + r)
            a = jnp.where(mask, jnp.exp(z + c), 0.0)
            o = o + jnp.dot(a.astype(BF16), v, preferred_element_type=F32)
            return o, c[:, 0:1]

        o, _ = lax.fori_loop(0, i + 1, body,
                             (jnp.zeros((TQ, HEAD_DIM), F32), jnp.zeros((TQ, 1), F32)))
        outs.append(o)
    o_ref[...] = jnp.concatenate(outs, axis=1)


def _attention(qkv3):
    b, s, _ = qkv3.shape
    hp = SB_HEADS // 2
    return pl.pallas_call(
        _attn_kernel,
        grid=(b, hp, s // TQ),
        in_specs=[
            pl.BlockSpec((None, TQ, LANES), lambda bi, h, i: (bi, i, h)),
            pl.BlockSpec((None, s, LANES), lambda bi, h, i: (bi, 0, hp + h)),
            pl.BlockSpec((None, s, LANES), lambda bi, h, i: (bi, 0, 2 * hp + h)),
        ],
        out_specs=pl.BlockSpec((None, TQ, LANES), lambda bi, h, i: (bi, i, h)),
        out_shape=jax.ShapeDtypeStruct((b, s, SB_WIDTH), F32),
        compiler_params=pltpu.CompilerParams(
            dimension_semantics=("parallel", "parallel", "arbitrary"),
            vmem_limit_bytes=VMEM_LIMIT),
        name="sb_attention",
    )(qkv3, qkv3, qkv3)


def _topk_rows(s, k):
    n = s.shape[0]
    iota = lax.broadcasted_iota(jnp.int32, s.shape, 0).astype(F32)
    vals, idxs = [], []
    for _ in range(k):
        m = jnp.max(s, axis=0, keepdims=True)
        im = jnp.min(jnp.where(s == m, iota, float(n)), axis=0, keepdims=True)
        vals.append(m)
        idxs.append(im)
        s = jnp.where(iota == im, -jnp.inf, s)
    return jnp.concatenate(vals, axis=0), jnp.concatenate(idxs, axis=0)


def _lookup_rows(sel, table):
    out = jnp.zeros_like(sel)
    for a in range(table.shape[0]):
        out = out + jnp.where(sel == float(a), table[a:a + 1, :], 0.0)
    return out


def _mix_route_kernel(tiles_per_seq, x_ref, oa_ref, gb_ref, gc_ref, hc_ref, gcp_ref, hcp_ref,
                      cw_ref, cb_ref, ag_ref, cg_ref, wo_ref, fg_ref, wq_ref, keys_ref,
                      x1_ref, xn_ref, idx_ref, gt_ref, q_sc, it_sc, gt_sc):
    i = pl.program_id(0)
    tm = x_ref.shape[0]
    u = gc_ref[...] * hc_ref[...]
    first = (i % tiles_per_seq) == 0
    u_prev = jnp.where(first, 0.0, gcp_ref[...] * hcp_ref[...])
    u_ext = jnp.concatenate([u_prev, u], axis=0)
    cw = cw_ref[...]
    y = (cw[0:1, :] * u_ext[SUBLANES - 2:SUBLANES - 2 + tm, :]
         + cw[1:2, :] * u_ext[SUBLANES - 1:SUBLANES - 1 + tm, :]
         + cw[2:3, :] * u) + cb_ref[...]
    o_conv = gb_ref[...] * y
    mixed = jnp.concatenate([_rms(oa_ref[...], ag_ref[...]), _rms(o_conv, cg_ref[...])], axis=1)
    x1 = x_ref[...] + jnp.dot(mixed.astype(BF16), wo_ref[...], preferred_element_type=F32)
    x1_ref[...] = x1
    xn = _rms(x1, fg_ref[...])
    xn_ref[...] = xn
    q_sc[...] = jnp.dot(xn.astype(BF16), wq_ref[...], preferred_element_type=F32)

    def head(h, carry):
        c0 = pl.multiple_of(h * PEER_DK, PEER_DK)
        q1 = q_sc[:, pl.ds(c0, D_HALF)].astype(BF16)
        q2 = q_sc[:, pl.ds(c0 + D_HALF, D_HALF)].astype(BF16)
        nt = (((1,), (1,)), ((), ()))
        s1 = lax.dot_general(keys_ref[0, h], q1, nt, preferred_element_type=F32)
        s2 = lax.dot_general(keys_ref[1, h], q2, nt, preferred_element_type=F32)
        v1, i1 = _topk_rows(s1, PEER_TOPK)
        v2, i2 = _topk_rows(s2, PEER_TOPK)
        cand = jnp.concatenate([v1[a:a + 1, :] + v2 for a in range(PEER_TOPK)], axis=0)
        best, pos = _topk_rows(cand, PEER_TOPK)
        a_sel = jnp.floor(pos * (1.0 / PEER_TOPK))
        b_sel = pos - a_sel * PEER_TOPK
        expert = _lookup_rows(a_sel, i1) * N_KEYS + _lookup_rows(b_sel, i2)
        e = jnp.exp(best - best[0:1, :])
        gate = e / jnp.sum(e, axis=0, keepdims=True)
        r0 = pl.multiple_of(h * PEER_TOPK, PEER_TOPK)
        it_sc[pl.ds(r0, PEER_TOPK), :] = expert
        gt_sc[pl.ds(r0, PEER_TOPK), :] = gate
        return carry

    lax.fori_loop(0, PEER_HEADS, head, 0)
    idx_ref[...] = it_sc[...].T.astype(jnp.int32)
    gt_ref[...] = gt_sc[...]


def _mix_route(x2, oa, cv, conv_w, conv_b, attn_g, conv_g, wo_bf16, ffn_g, wq_bf16, keys_bf16, seq):
    m = x2.shape[0]
    tm = TM_MIX
    row = lambda i: (i, 0)
    const2 = lambda i: (0, 0)
    halo = tm // SUBLANES
    return pl.pallas_call(
        functools.partial(_mix_route_kernel, seq // tm),
        grid=(m // tm,),
        in_specs=[
            pl.BlockSpec((tm, D_MODEL), row),
            pl.BlockSpec((tm, SB_WIDTH), row),
            pl.BlockSpec((tm, CONV_WIDTH), lambda i: (i, 0)),
            pl.BlockSpec((tm, CONV_WIDTH), lambda i: (i, 1)),
            pl.BlockSpec((tm, CONV_WIDTH), lambda i: (i, 2)),
            pl.BlockSpec((SUBLANES, CONV_WIDTH), lambda i: (jnp.maximum(i * halo - 1, 0), 1)),
            pl.BlockSpec((SUBLANES, CONV_WIDTH), lambda i: (jnp.maximum(i * halo - 1, 0), 2)),
            pl.BlockSpec((CONV_K, CONV_WIDTH), const2),
            pl.BlockSpec((1, CONV_WIDTH), const2),
            pl.BlockSpec((1, SB_WIDTH), const2),
            pl.BlockSpec((1, CONV_WIDTH), const2),
            pl.BlockSpec((D_MODEL, D_MODEL), const2),
            pl.BlockSpec((1, D_MODEL), const2),
            pl.BlockSpec((D_MODEL, PEER_HEADS * PEER_DK), const2),
            pl.BlockSpec((2, PEER_HEADS, N_KEYS, D_HALF), lambda i: (0, 0, 0, 0)),
        ],
        out_specs=[
            pl.BlockSpec((tm, D_MODEL), row),
            pl.BlockSpec((tm, D_MODEL), row),
            pl.BlockSpec((tm, N_PICKS), row),
            pl.BlockSpec((N_PICKS, tm), lambda i: (0, i)),
        ],
        out_shape=[
            jax.ShapeDtypeStruct((m, D_MODEL), F32),
            jax.ShapeDtypeStruct((m, D_MODEL), F32),
            jax.ShapeDtypeStruct((m, N_PICKS), jnp.int32),
            jax.ShapeDtypeStruct((N_PICKS, m), F32),
        ],
        scratch_shapes=[
            pltpu.VMEM((tm, PEER_HEADS * PEER_DK), F32),
            pltpu.VMEM((N_PICKS, tm), F32),
            pltpu.VMEM((N_PICKS, tm), F32),
        ],
        compiler_params=pltpu.CompilerParams(
            dimension_semantics=("parallel",), vmem_limit_bytes=VMEM_LIMIT),
        name="mix_route",
    )(x2, oa, cv, cv, cv, cv, cv, conv_w, conv_b.reshape(1, -1), attn_g.reshape(1, -1),
      conv_g.reshape(1, -1), wo_bf16, ffn_g.reshape(1, -1), wq_bf16, keys_bf16)


def _peer_kernel(idx_ref, idxn_ref, xn_ref, gt_ref, u_hbm, v_hbm, out_ref, ubuf, vbuf, sem, w_sc):
    i = pl.program_id(0)
    n = pl.num_programs(0)
    slot = i % 2
    rows = T_PEER * N_PICKS

    def issue(src_idx, s):
        def body(r, carry):
            e = src_idx[r]
            pltpu.make_async_copy(u_hbm.at[e], ubuf.at[s, r], sem.at[0, s]).start()
            pltpu.make_async_copy(v_hbm.at[e], vbuf.at[s, r], sem.at[1, s]).start()
            return carry
        lax.fori_loop(0, rows, body, 0)

    @pl.when(i == 0)
    def _():
        issue(idx_ref, 0)

    @pl.when(i + 1 < n)
    def _():
        issue(idxn_ref, 1 - slot)

    pltpu.make_async_copy(u_hbm.at[pl.ds(0, rows)], ubuf.at[slot], sem.at[0, slot]).wait()
    pltpu.make_async_copy(v_hbm.at[pl.ds(0, rows)], vbuf.at[slot], sem.at[1, slot]).wait()

    ones = jnp.ones((LANES, LANES), BF16)
    lane = lax.broadcasted_iota(jnp.int32, (N_PICKS, LANES), 1)
    lane0 = (i * T_PEER) % LANES
    gt = gt_ref[...]
    for t in range(T_PEER):
        x_t = xn_ref[t]
        u = ubuf[slot, pl.ds(t * N_PICKS, N_PICKS)]
        prod = (u * x_t[None]).reshape(N_PICKS * SUBLANES, LANES)
        p_hi = prod.astype(BF16)
        p_lo = (prod - p_hi.astype(F32)).astype(BF16)
        lane_sum = (jnp.dot(p_hi, ones, preferred_element_type=F32)
                    + jnp.dot(p_lo, ones, preferred_element_type=F32))
        act = jnp.sum(lane_sum.reshape(N_PICKS, SUBLANES, LANES), axis=1)
        gate = jnp.sum(jnp.where(lane == lane0 + t, gt, 0.0), axis=1, keepdims=True)
        w_sc[...] = jax.nn.gelu(act) * gate
        accs = [jnp.zeros((SUBLANES, LANES), F32) for _ in range(4)]
        for k in range(N_PICKS):
            wk = jnp.broadcast_to(w_sc[k:k + 1, :], (SUBLANES, LANES))
            accs[k % 4] = accs[k % 4] + wk * vbuf[slot, t * N_PICKS + k]
        out_ref[t] = (accs[0] + accs[1]) + (accs[2] + accs[3])


def _peer_experts(idx_flat, xn3, gt, u3, v3):
    m = xn3.shape[0]
    rows = T_PEER * N_PICKS
    n = m // T_PEER
    tile = (SUBLANES, LANES)
    return pl.pallas_call(
        _peer_kernel,
        grid=(n,),
        in_specs=[
            pl.BlockSpec((rows,), lambda i: (i,), memory_space=pltpu.SMEM),
            pl.BlockSpec((rows,), lambda i: (jnp.minimum(i + 1, n - 1),), memory_space=pltpu.SMEM),
            pl.BlockSpec((T_PEER,) + tile, lambda i: (i, 0, 0)),
            pl.BlockSpec((N_PICKS, LANES), lambda i: (0, (i * T_PEER) // LANES)),
            pl.BlockSpec(memory_space=pl.ANY),
            pl.BlockSpec(memory_space=pl.ANY),
        ],
        out_specs=pl.BlockSpec((T_PEER,) + tile, lambda i: (i, 0, 0)),
        out_shape=jax.ShapeDtypeStruct((m,) + tile, F32),
        scratch_shapes=[
            pltpu.VMEM((2, rows) + tile, F32),
            pltpu.VMEM((2, rows) + tile, F32),
            pltpu.SemaphoreType.DMA((2, 2)),
            pltpu.VMEM((N_PICKS, LANES), F32),
        ],
        compiler_params=pltpu.CompilerParams(
            dimension_semantics=("arbitrary",), vmem_limit_bytes=VMEM_LIMIT),
        name="peer_experts",
    )(idx_flat, idx_flat, xn3, gt, u3, v3)


def _final_kernel(x1_ref, p_ref, g_ref, o_ref):
    o_ref[...] = _rms(x1_ref[...] + p_ref[...], g_ref[...])


def _residual_norm(x1, p, g):
    m = x1.shape[0]
    row = lambda i: (i, 0)
    return pl.pallas_call(
        _final_kernel,
        grid=(m // TM_FINAL,),
        in_specs=[pl.BlockSpec((TM_FINAL, D_MODEL), row), pl.BlockSpec((TM_FINAL, D_MODEL), row),
                  pl.BlockSpec((1, D_MODEL), lambda i: (0, 0))],
        out_specs=pl.BlockSpec((TM_FINAL, D_MODEL), row),
        out_shape=jax.ShapeDtypeStruct((m, D_MODEL), F32),
        compiler_params=pltpu.CompilerParams(dimension_semantics=("parallel",)),
        name="residual_norm",
    )(x1, p, g.reshape(1, D_MODEL))


def kernel(x, mix_norm_g, w_in, conv_w, conv_b, attn_out_g, conv_out_g, w_out, ffn_norm_g,
           peer_wq, peer_keys, peer_u, peer_v, final_norm_g):
    b, s, d = x.shape
    m = b * s
    depth = w_in.shape[0]
    n_exp = peer_u.shape[1]
    x2 = x.reshape(m, d)
    ones_g = jnp.ones((d,), F32)
    for layer in range(depth):
        qkv, cv = _inproj(x2, mix_norm_g[layer], w_in[layer].astype(BF16))
        oa = _attention(qkv.reshape(b, s, 3 * SB_WIDTH)).reshape(m, SB_WIDTH)
        x1, xn, idx, gt = _mix_route(
            x2, oa, cv, conv_w[layer], conv_b[layer], attn_out_g[layer], conv_out_g[layer],
            w_out[layer].astype(BF16), ffn_norm_g[layer], peer_wq[layer].astype(BF16),
            peer_keys[layer].astype(BF16), s)
        p3 = _peer_experts(idx.reshape(m * N_PICKS), xn.reshape(m, SUBLANES, LANES), gt,
                           peer_u[layer].reshape(n_exp, SUBLANES, LANES),
                           peer_v[layer].reshape(n_exp, SUBLANES, LANES))
        last = layer == depth - 1
        x2 = _residual_norm(x1, p3.reshape(m, d), final_norm_g if last else ones_g)
        if not last:
            raise NotImplementedError("depth > 1 needs an un-normalised residual output")
    return x2.reshape(b, s, d)
```

```python
import functools
import math

import jax
import jax.numpy as jnp
from jax import lax
from jax.experimental import pallas as pl
from jax.experimental.pallas import tpu as pltpu
from jax.experimental.pallas import tpu_sc as plsc

F32 = jnp.float32
BF16 = jnp.bfloat16

D_MODEL = 1024
SB_HEADS = 8
HEAD_DIM = 64
SB_WIDTH = SB_HEADS * HEAD_DIM
CONV_WIDTH = D_MODEL - SB_WIDTH
CONV_K = 3
N_KEYS = 128
PEER_HEADS = 8
PEER_DK = 256
D_HALF = PEER_DK // 2
PEER_TOPK = 16
N_PICKS = PEER_HEADS * PEER_TOPK
EPS = 1e-6

SUBLANES = 8
LANES = 128
ROW_TILES = D_MODEL // LANES

TM_PROJ = 512
TQ = 128
TM_MIX = 256
SC_CORES = 2
SC_SUBCORES = 16
SC_WORKERS = SC_CORES * SC_SUBCORES
SC_LANES = 16
SC_GROUP = 8
TM_FINAL = 512
VMEM_LIMIT = 56 * 1024 * 1024


def _rms(x, g):
    return x * lax.rsqrt(jnp.mean(x * x, axis=-1, keepdims=True) + EPS) * g


def _inproj_kernel(x_ref, g_ref, w_ref, qkv_ref, cv_ref):
    h = _rms(x_ref[...], g_ref[...])
    p = jnp.dot(h.astype(BF16), w_ref[...], preferred_element_type=F32)
    qkv_ref[...] = p[:, : 3 * SB_WIDTH].astype(BF16)
    cv_ref[...] = p[:, 3 * SB_WIDTH:]


def _inproj(x2, g, w_bf16):
    m = x2.shape[0]
    n = w_bf16.shape[1]
    return pl.pallas_call(
        _inproj_kernel,
        grid=(m // TM_PROJ,),
        in_specs=[
            pl.BlockSpec((TM_PROJ, D_MODEL), lambda i: (i, 0)),
            pl.BlockSpec((1, D_MODEL), lambda i: (0, 0)),
            pl.BlockSpec((D_MODEL, n), lambda i: (0, 0)),
        ],
        out_specs=[
            pl.BlockSpec((TM_PROJ, 3 * SB_WIDTH), lambda i: (i, 0)),
            pl.BlockSpec((TM_PROJ, 3 * CONV_WIDTH), lambda i: (i, 0)),
        ],
        out_shape=[
            jax.ShapeDtypeStruct((m, 3 * SB_WIDTH), BF16),
            jax.ShapeDtypeStruct((m, 3 * CONV_WIDTH), F32),
        ],
        compiler_params=pltpu.CompilerParams(
            dimension_semantics=("parallel",), vmem_limit_bytes=VMEM_LIMIT),
        name="inproj",
    )(x2, g.reshape(1, D_MODEL), w_bf16)


def _softplus(z):
    return jnp.maximum(z, 0.0) + jnp.log1p(jnp.exp(-jnp.abs(z)))


def _attn_kernel(q_ref, k_ref, v_ref, o_ref):
    i = pl.program_id(2)
    scale = 1.0 / math.sqrt(HEAD_DIM)
    rows = lax.broadcasted_iota(jnp.int32, (TQ, TQ), 0)
    cols = lax.broadcasted_iota(jnp.int32, (TQ, TQ), 1)
    tri = (rows >= cols).astype(BF16)
    outs = []
    for hh in range(2):
        lo = hh * HEAD_DIM
        q = q_ref[:, lo:lo + HEAD_DIM]

        def body(jj, carry, q=q, lo=lo):
            o, r = carry
            j = i - jj
            s0 = pl.multiple_of(j * TQ, TQ)
            k = k_ref[pl.ds(s0, TQ), lo:lo + HEAD_DIM]
            v = v_ref[pl.ds(s0, TQ), lo:lo + HEAD_DIM]
            z = lax.dot_general(q, k, (((1,), (1,)), ((), ())), preferred_element_type=F32) * scale
            mask = (rows + jj * TQ) > cols
            l1m = jnp.where(mask, -_softplus(z), 0.0)
            l_hi = l1m.astype(BF16)
            l_lo = (l1m - l_hi.astype(F32)).astype(BF16)
            c = (jnp.dot(l_hi, tri, preferred_element_type=F32)
                 + jnp.dot(l_lo, tri, preferred_element_type=F32) + r)
            a = jnp.where(mask, jnp.exp(z + c), 0.0)
            o = o + jnp.dot(a.astype(BF16), v, preferred_element_type=F32)
            return o, c[:, 0:1]

        o, _ = lax.fori_loop(0, i + 1, body,
                             (jnp.zeros((TQ, HEAD_DIM), F32), jnp.zeros((TQ, 1), F32)))
        outs.append(o)
    o_ref[...] = jnp.concatenate(outs, axis=1)


def _attention(qkv3):
    b, s, _ = qkv3.shape
    hp = SB_HEADS // 2
    return pl.pallas_call(
        _attn_kernel,
        grid=(b, hp, s // TQ),
        in_specs=[
            pl.BlockSpec((None, TQ, LANES), lambda bi, h, i: (bi, i, h)),
            pl.BlockSpec((None, s, LANES), lambda bi, h, i: (bi, 0, hp + h)),
            pl.BlockSpec((None, s, LANES), lambda bi, h, i: (bi, 0, 2 * hp + h)),
        ],
        out_specs=pl.BlockSpec((None, TQ, LANES), lambda bi, h, i: (bi, i, h)),
        out_shape=jax.ShapeDtypeStruct((b, s, SB_WIDTH), F32),
        compiler_params=pltpu.CompilerParams(
            dimension_semantics=("parallel", "parallel", "arbitrary"),
            vmem_limit_bytes=VMEM_LIMIT),
        name="sb_attention",
    )(qkv3, qkv3, qkv3)


def _topk_rows(s, k):
    n = s.shape[0]
    iota = lax.broadcasted_iota(jnp.int32, s.shape, 0).astype(F32)
    vals, idxs = [], []
    for _ in range(k):
        m = jnp.max(s, axis=0, keepdims=True)
        im = jnp.min(jnp.where(s == m, iota, float(n)), axis=0, keepdims=True)
        vals.append(m)
        idxs.append(im)
        s = jnp.where(iota == im, -jnp.inf, s)
    return jnp.concatenate(vals, axis=0), jnp.concatenate(idxs, axis=0)


def _lookup_rows(sel, table):
    out = jnp.zeros_like(sel)
    for a in range(table.shape[0]):
        out = out + jnp.where(sel == float(a), table[a:a + 1, :], 0.0)
    return out


def _mix_route_kernel(tiles_per_seq, x_ref, oa_ref, gb_ref, gc_ref, hc_ref, gcp_ref, hcp_ref,
                      cw_ref, cb_ref, ag_ref, cg_ref, wo_ref, fg_ref, wq_ref, keys_ref,
                      x1_ref, xn_ref, idx_ref, gt_ref, q_sc, it_sc, gt_sc):
    i = pl.program_id(0)
    tm = x_ref.shape[0]
    u = gc_ref[...] * hc_ref[...]
    first = (i % tiles_per_seq) == 0
    u_prev = jnp.where(first, 0.0, gcp_ref[...] * hcp_ref[...])
    u_ext = jnp.concatenate([u_prev, u], axis=0)
    cw = cw_ref[...]
    y = (cw[0:1, :] * u_ext[SUBLANES - 2:SUBLANES - 2 + tm, :]
         + cw[1:2, :] * u_ext[SUBLANES - 1:SUBLANES - 1 + tm, :]
         + cw[2:3, :] * u) + cb_ref[...]
    o_conv = gb_ref[...] * y
    mixed = jnp.concatenate([_rms(oa_ref[...], ag_ref[...]), _rms(o_conv, cg_ref[...])], axis=1)
    x1 = x_ref[...] + jnp.dot(mixed.astype(BF16), wo_ref[...], preferred_element_type=F32)
    x1_ref[...] = x1
    xn = _rms(x1, fg_ref[...])
    xn_ref[...] = xn
    q_sc[...] = jnp.dot(xn.astype(BF16), wq_ref[...], preferred_element_type=F32)

    def head(h, carry):
        c0 = pl.multiple_of(h * PEER_DK, PEER_DK)
        q1 = q_sc[:, pl.ds(c0, D_HALF)].astype(BF16)
        q2 = q_sc[:, pl.ds(c0 + D_HALF, D_HALF)].astype(BF16)
        nt = (((1,), (1,)), ((), ()))
        s1 = lax.dot_general(keys_ref[0, h], q1, nt, preferred_element_type=F32)
        s2 = lax.dot_general(keys_ref[1, h], q2, nt, preferred_element_type=F32)
        v1, i1 = _topk_rows(s1, PEER_TOPK)
        v2, i2 = _topk_rows(s2, PEER_TOPK)
        cand = jnp.concatenate([v1[a:a + 1, :] + v2 for a in range(PEER_TOPK)], axis=0)
        best, pos = _topk_rows(cand, PEER_TOPK)
        a_sel = jnp.floor(pos * (1.0 / PEER_TOPK))
        b_sel = pos - a_sel * PEER_TOPK
        expert = _lookup_rows(a_sel, i1) * N_KEYS + _lookup_rows(b_sel, i2)
        e = jnp.exp(best - best[0:1, :])
        gate = e / jnp.sum(e, axis=0, keepdims=True)
        r0 = pl.multiple_of(h * PEER_TOPK, PEER_TOPK)
        it_sc[pl.ds(r0, PEER_TOPK), :] = expert
        gt_sc[pl.ds(r0, PEER_TOPK), :] = gate
        return carry

    lax.fori_loop(0, PEER_HEADS, head, 0)
    idx_ref[...] = it_sc[...].T.astype(jnp.int32)
    gt_ref[...] = gt_sc[...].T


def _mix_route(x2, oa, cv, conv_w, conv_b, attn_g, conv_g, wo_bf16, ffn_g, wq_bf16, keys_bf16, seq):
    m = x2.shape[0]
    tm = TM_MIX
    row = lambda i: (i, 0)
    const2 = lambda i: (0, 0)
    halo = tm // SUBLANES
    return pl.pallas_call(
        functools.partial(_mix_route_kernel, seq // tm),
        grid=(m // tm,),
        in_specs=[
            pl.BlockSpec((tm, D_MODEL), row),
            pl.BlockSpec((tm, SB_WIDTH), row),
            pl.BlockSpec((tm, CONV_WIDTH), lambda i: (i, 0)),
            pl.BlockSpec((tm, CONV_WIDTH), lambda i: (i, 1)),
            pl.BlockSpec((tm, CONV_WIDTH), lambda i: (i, 2)),
            pl.BlockSpec((SUBLANES, CONV_WIDTH), lambda i: (jnp.maximum(i * halo - 1, 0), 1)),
            pl.BlockSpec((SUBLANES, CONV_WIDTH), lambda i: (jnp.maximum(i * halo - 1, 0), 2)),
            pl.BlockSpec((CONV_K, CONV_WIDTH), const2),
            pl.BlockSpec((1, CONV_WIDTH), const2),
            pl.BlockSpec((1, SB_WIDTH), const2),
            pl.BlockSpec((1, CONV_WIDTH), const2),
            pl.BlockSpec((D_MODEL, D_MODEL), const2),
            pl.BlockSpec((1, D_MODEL), const2),
            pl.BlockSpec((D_MODEL, PEER_HEADS * PEER_DK), const2),
            pl.BlockSpec((2, PEER_HEADS, N_KEYS, D_HALF), lambda i: (0, 0, 0, 0)),
        ],
        out_specs=[
            pl.BlockSpec((tm, D_MODEL), row),
            pl.BlockSpec((tm, D_MODEL), row),
            pl.BlockSpec((tm, N_PICKS), row),
            pl.BlockSpec((tm, N_PICKS), row),
        ],
        out_shape=[
            jax.ShapeDtypeStruct((m, D_MODEL), F32),
            jax.ShapeDtypeStruct((m, D_MODEL), F32),
            jax.ShapeDtypeStruct((m, N_PICKS), jnp.int32),
            jax.ShapeDtypeStruct((m, N_PICKS), F32),
        ],
        scratch_shapes=[
            pltpu.VMEM((tm, PEER_HEADS * PEER_DK), F32),
            pltpu.VMEM((N_PICKS, tm), F32),
            pltpu.VMEM((N_PICKS, tm), F32),
        ],
        compiler_params=pltpu.CompilerParams(
            dimension_semantics=("parallel",), vmem_limit_bytes=VMEM_LIMIT),
        name="mix_route",
    )(x2, oa, cv, cv, cv, cv, cv, conv_w, conv_b.reshape(1, -1), attn_g.reshape(1, -1),
      conv_g.reshape(1, -1), wo_bf16, ffn_g.reshape(1, -1), wq_bf16, keys_bf16)


def _gelu_tanh(y):
    inner = math.sqrt(2.0 / math.pi) * (y + 0.044715 * (y * y * y))
    th = 1.0 - 2.0 / (jnp.exp(2.0 * inner) + 1.0)
    return 0.5 * y * (1.0 + th)


def _peer_sc_kernel(idx_hbm, g_hbm, x_hbm, u_hbm, v_hbm, out_hbm,
                    idx_v, g_v, x_v, out_v, ubuf, vbuf, red_v, w_v, usem, vsem):
    tok_per_w = x_hbm.shape[0] // SC_WORKERS
    n_groups = tok_per_w // SC_GROUP
    n_chunks = SC_GROUP * PEER_HEADS
    d_chunks = D_MODEL // SC_LANES
    wid = lax.axis_index("s") * SC_CORES + lax.axis_index("c")
    lane = lax.iota(jnp.int32, SC_LANES)
    zero = jnp.zeros((SC_LANES,), F32)

    def copies(j, b):
        return (pltpu.make_async_copy(u_hbm.at[idx_v.at[j]], ubuf.at[b], usem.at[b]),
                pltpu.make_async_copy(v_hbm.at[idx_v.at[j]], vbuf.at[b], vsem.at[b]))

    def compute(j, b):
        tl = j // PEER_HEADS

        def dot_body(c, accs):
            off = pl.multiple_of(c * SC_LANES, SC_LANES)
            xc = x_v[tl, pl.ds(off, SC_LANES)]
            return tuple(accs[k] + ubuf[b, k, pl.ds(off, SC_LANES)] * xc for k in range(PEER_TOPK))

        accs = lax.fori_loop(0, d_chunks, dot_body, tuple(zero for _ in range(PEER_TOPK)))
        for k in range(PEER_TOPK):
            red_v[k, pl.ds(SC_LANES, SC_LANES)] = accs[k]
        act = zero
        for l in range(SC_LANES):
            act = act + plsc.load_gather(red_v, [lane, jnp.full((SC_LANES,), SC_LANES + l, jnp.int32)])
        w_v[pl.ds(SC_LANES, SC_LANES)] = g_v[j, :] * _gelu_tanh(act)
        wb = [plsc.load_gather(w_v, [jnp.full((SC_LANES,), SC_LANES + k, jnp.int32)])
              for k in range(PEER_TOPK)]

        def acc_body(c, carry):
            off = pl.multiple_of(c * SC_LANES, SC_LANES)
            terms = [wb[k] * vbuf[b, k, pl.ds(off, SC_LANES)] for k in range(PEER_TOPK)]
            while len(terms) > 1:
                terms = [terms[p] + terms[p + 1] for p in range(0, len(terms), 2)]
            out_v[tl, pl.ds(off, SC_LANES)] = out_v[tl, pl.ds(off, SC_LANES)] + terms[0]
            return carry

        lax.fori_loop(0, d_chunks, acc_body, 0)

    def group(gi, carry):
        tok0 = pl.multiple_of(wid * tok_per_w + gi * SC_GROUP, SC_GROUP)
        row0 = pl.multiple_of(tok0 * PEER_HEADS, SC_GROUP * PEER_HEADS)
        pltpu.sync_copy(idx_hbm.at[pl.ds(row0, n_chunks)], idx_v)
        pltpu.sync_copy(g_hbm.at[pl.ds(row0, n_chunks)], g_v)
        pltpu.sync_copy(x_hbm.at[pl.ds(tok0, SC_GROUP)], x_v)

        def zero_body(c, carry2):
            off = pl.multiple_of(c * SC_LANES, SC_LANES)
            for t in range(SC_GROUP):
                out_v[t, pl.ds(off, SC_LANES)] = zero
            return carry2

        lax.fori_loop(0, d_chunks, zero_body, 0)
        for cp in copies(0, 0):
            cp.start()

        def pair(jj, carry2):
            for b in range(2):
                j = jj * 2 + b

                @pl.when(j + 1 < n_chunks)
                def _():
                    for cp in copies(j + 1, 1 - b):
                        cp.start()

                for cp in copies(j, b):
                    cp.wait()
                compute(j, b)
            return carry2

        lax.fori_loop(0, n_chunks // 2, pair, 0)
        pltpu.sync_copy(out_v, out_hbm.at[pl.ds(tok0, SC_GROUP)])
        return carry

    lax.fori_loop(0, n_groups, group, 0)


def _peer_experts(idx2, g2, xn, u, v):
    m = xn.shape[0]
    n_chunks = SC_GROUP * PEER_HEADS
    mesh = plsc.VectorSubcoreMesh(core_axis_name="c", subcore_axis_name="s")
    return pl.kernel(
        _peer_sc_kernel,
        out_type=jax.ShapeDtypeStruct((m, D_MODEL), F32),
        mesh=mesh,
        scratch_types=[
            pltpu.VMEM((n_chunks, PEER_TOPK), jnp.int32),
            pltpu.VMEM((n_chunks, PEER_TOPK), F32),
            pltpu.VMEM((SC_GROUP, D_MODEL), F32),
            pltpu.VMEM((SC_GROUP, D_MODEL), F32),
            pltpu.VMEM((2, PEER_TOPK, D_MODEL), F32),
            pltpu.VMEM((2, PEER_TOPK, D_MODEL), F32),
            pltpu.VMEM((PEER_TOPK, 2 * SC_LANES), F32),
            pltpu.VMEM((2 * SC_LANES,), F32),
            pltpu.SemaphoreType.DMA((2,)),
            pltpu.SemaphoreType.DMA((2,)),
        ],
        compiler_params=pltpu.CompilerParams(needs_layout_passes=False),
        name="peer_experts_sc",
    )(idx2, g2, xn, u, v)


def _final_kernel(x1_ref, p_ref, g_ref, o_ref):
    o_ref[...] = _rms(x1_ref[...] + p_ref[...], g_ref[...])


def _residual_norm(x1, p, g):
    m = x1.shape[0]
    row = lambda i: (i, 0)
    return pl.pallas_call(
        _final_kernel,
        grid=(m // TM_FINAL,),
        in_specs=[pl.BlockSpec((TM_FINAL, D_MODEL), row), pl.BlockSpec((TM_FINAL, D_MODEL), row),
                  pl.BlockSpec((1, D_MODEL), lambda i: (0, 0))],
        out_specs=pl.BlockSpec((TM_FINAL, D_MODEL), row),
        out_shape=jax.ShapeDtypeStruct((m, D_MODEL), F32),
        compiler_params=pltpu.CompilerParams(dimension_semantics=("parallel",)),
        name="residual_norm",
    )(x1, p, g.reshape(1, D_MODEL))


def kernel(x, mix_norm_g, w_in, conv_w, conv_b, attn_out_g, conv_out_g, w_out, ffn_norm_g,
           peer_wq, peer_keys, peer_u, peer_v, final_norm_g):
    b, s, d = x.shape
    m = b * s
    depth = w_in.shape[0]
    n_exp = peer_u.shape[1]
    x2 = x.reshape(m, d)
    ones_g = jnp.ones((d,), F32)
    for layer in range(depth):
        qkv, cv = _inproj(x2, mix_norm_g[layer], w_in[layer].astype(BF16))
        oa = _attention(qkv.reshape(b, s, 3 * SB_WIDTH)).reshape(m, SB_WIDTH)
        x1, xn, idx, gt = _mix_route(
            x2, oa, cv, conv_w[layer], conv_b[layer], attn_out_g[layer], conv_out_g[layer],
            w_out[layer].astype(BF16), ffn_norm_g[layer], peer_wq[layer].astype(BF16),
            peer_keys[layer].astype(BF16), s)
        p3 = _peer_experts(idx.reshape(m * PEER_HEADS, PEER_TOPK), gt.reshape(m * PEER_HEADS, PEER_TOPK),
                           xn, peer_u[layer], peer_v[layer])
        last = layer == depth - 1
        x2 = _residual_norm(x1, p3.reshape(m, d), final_norm_g if last else ones_g)
        if not last:
            raise NotImplementedError("depth > 1 needs an un-normalised residual output")
    return x2.reshape(b, s, d)
```

```python
import functools
import math

import jax
import jax.numpy as jnp
from jax import lax
from jax.experimental import pallas as pl
from jax.experimental.pallas import tpu as pltpu
from jax.experimental.pallas import tpu_sc as plsc

F32 = jnp.float32
BF16 = jnp.bfloat16

D_MODEL = 1024
SB_HEADS = 8
HEAD_DIM = 64
SB_WIDTH = SB_HEADS * HEAD_DIM
CONV_WIDTH = D_MODEL - SB_WIDTH
CONV_K = 3
N_KEYS = 128
PEER_HEADS = 8
PEER_DK = 256
D_HALF = PEER_DK // 2
PEER_TOPK = 16
N_PICKS = PEER_HEADS * PEER_TOPK
EPS = 1e-6

SUBLANES = 8
LANES = 128
ROW_TILES = D_MODEL // LANES

TM_PROJ = 512
TQ = 256
TM_MIX = 256
SC_CORES = 2
SC_SUBCORES = 16
SC_WORKERS = SC_CORES * SC_SUBCORES
SC_LANES = 16
SC_GROUP = 8
TM_FINAL = 512
BATCH_CHUNKS = 4
VMEM_LIMIT = 56 * 1024 * 1024


def _rms(x, g):
    return x * lax.rsqrt(jnp.mean(x * x, axis=-1, keepdims=True) + EPS) * g


def _inproj_kernel(x_ref, g_ref, w_ref, qkv_ref, cv_ref):
    h = _rms(x_ref[...], g_ref[...])
    p = jnp.dot(h.astype(BF16), w_ref[...], preferred_element_type=F32)
    qkv_ref[...] = p[:, : 3 * SB_WIDTH].astype(BF16)
    cv_ref[...] = p[:, 3 * SB_WIDTH:]


def _inproj(x2, g, w_bf16):
    m = x2.shape[0]
    n = w_bf16.shape[1]
    return pl.pallas_call(
        _inproj_kernel,
        grid=(m // TM_PROJ,),
        in_specs=[
            pl.BlockSpec((TM_PROJ, D_MODEL), lambda i: (i, 0)),
            pl.BlockSpec((1, D_MODEL), lambda i: (0, 0)),
            pl.BlockSpec((D_MODEL, n), lambda i: (0, 0)),
        ],
        out_specs=[
            pl.BlockSpec((TM_PROJ, 3 * SB_WIDTH), lambda i: (i, 0)),
            pl.BlockSpec((TM_PROJ, 3 * CONV_WIDTH), lambda i: (i, 0)),
        ],
        out_shape=[
            jax.ShapeDtypeStruct((m, 3 * SB_WIDTH), BF16),
            jax.ShapeDtypeStruct((m, 3 * CONV_WIDTH), F32),
        ],
        compiler_params=pltpu.CompilerParams(
            dimension_semantics=("parallel",), vmem_limit_bytes=VMEM_LIMIT),
        name="inproj",
    )(x2, g.reshape(1, D_MODEL), w_bf16)


def _softplus(z):
    return jnp.maximum(z, 0.0) + jnp.log(1.0 + jnp.exp(-jnp.abs(z)))


def _attn_kernel(q_ref, k_ref, v_ref, o_ref):
    i = pl.program_id(2)
    scale = 1.0 / math.sqrt(HEAD_DIM)
    rows = lax.broadcasted_iota(jnp.int32, (TQ, TQ), 0)
    cols = lax.broadcasted_iota(jnp.int32, (TQ, TQ), 1)
    tri = (rows >= cols).astype(BF16)
    causal = rows > cols
    heads = [slice(hh * HEAD_DIM, (hh + 1) * HEAD_DIM) for hh in range(2)]
    qs = [q_ref[:, sl] * scale for sl in heads]

    def block(q, s0, sl, o, r, mask):
        k = k_ref[pl.ds(s0, TQ), sl]
        v = v_ref[pl.ds(s0, TQ), sl]
        z = lax.dot_general(q, k, (((1,), (1,)), ((), ())), preferred_element_type=F32)
        l1m = -_softplus(z)
        if mask is not None:
            l1m = jnp.where(mask, l1m, 0.0)
        l_hi = l1m.astype(BF16)
        l_lo = (l1m - l_hi.astype(F32)).astype(BF16)
        c = (jnp.dot(l_hi, tri, preferred_element_type=F32)
             + jnp.dot(l_lo, tri, preferred_element_type=F32) + r)
        a = jnp.exp(z + c)
        if mask is not None:
            a = jnp.where(mask, a, 0.0)
        o = o + jnp.dot(a.astype(BF16), v, preferred_element_type=F32)
        return o, c[:, 0:1]

    zero_o = jnp.zeros((TQ, HEAD_DIM), F32)
    zero_r = jnp.zeros((TQ, 1), F32)
    diag0 = pl.multiple_of(i * TQ, TQ)
    carry = tuple(block(qs[hh], diag0, heads[hh], zero_o, zero_r, causal) for hh in range(2))

    def body(jj, carry):
        s0 = pl.multiple_of((i - jj) * TQ, TQ)
        return tuple(block(qs[hh], s0, heads[hh], carry[hh][0], carry[hh][1], None)
                     for hh in range(2))

    carry = lax.fori_loop(1, i + 1, body, carry)
    o_ref[...] = jnp.concatenate([carry[0][0], carry[1][0]], axis=1)


def _attention(qkv3):
    b, s, _ = qkv3.shape
    hp = SB_HEADS // 2
    return pl.pallas_call(
        _attn_kernel,
        grid=(b, hp, s // TQ),
        in_specs=[
            pl.BlockSpec((None, TQ, LANES), lambda bi, h, i: (bi, i, h)),
            pl.BlockSpec((None, s, LANES), lambda bi, h, i: (bi, 0, hp + h)),
            pl.BlockSpec((None, s, LANES), lambda bi, h, i: (bi, 0, 2 * hp + h)),
        ],
        out_specs=pl.BlockSpec((None, TQ, LANES), lambda bi, h, i: (bi, i, h)),
        out_shape=jax.ShapeDtypeStruct((b, s, SB_WIDTH), F32),
        compiler_params=pltpu.CompilerParams(
            dimension_semantics=("parallel", "parallel", "arbitrary"),
            vmem_limit_bytes=VMEM_LIMIT),
        name="sb_attention",
    )(qkv3, qkv3, qkv3)


def _topk_rows(s, k):
    n = s.shape[0]
    iota = lax.broadcasted_iota(jnp.int32, s.shape, 0).astype(F32)
    vals, idxs = [], []
    for _ in range(k):
        m = jnp.max(s, axis=0, keepdims=True)
        im = jnp.min(jnp.where(s == m, iota, float(n)), axis=0, keepdims=True)
        vals.append(m)
        idxs.append(im)
        s = jnp.where(iota == im, -jnp.inf, s)
    return jnp.concatenate(vals, axis=0), jnp.concatenate(idxs, axis=0)


def _lookup_rows(sel, table):
    out = jnp.zeros_like(sel)
    for a in range(table.shape[0]):
        out = out + jnp.where(sel == float(a), table[a:a + 1, :], 0.0)
    return out


def _mix_route_kernel(tiles_per_seq, x_ref, oa_ref, gb_ref, gc_ref, hc_ref, gcp_ref, hcp_ref,
                      cw_ref, cb_ref, ag_ref, cg_ref, wo_ref, fg_ref, wq_ref, keys_ref,
                      x1_ref, xn_ref, idx_ref, gt_ref, q_sc, it_sc, gt_sc):
    i = pl.program_id(0)
    tm = x_ref.shape[0]
    u = gc_ref[...] * hc_ref[...]
    first = (i % tiles_per_seq) == 0
    u_prev = jnp.where(first, 0.0, gcp_ref[...] * hcp_ref[...])
    u_ext = jnp.concatenate([u_prev, u], axis=0)
    cw = cw_ref[...]
    y = (cw[0:1, :] * u_ext[SUBLANES - 2:SUBLANES - 2 + tm, :]
         + cw[1:2, :] * u_ext[SUBLANES - 1:SUBLANES - 1 + tm, :]
         + cw[2:3, :] * u) + cb_ref[...]
    o_conv = gb_ref[...] * y
    mixed = jnp.concatenate([_rms(oa_ref[...], ag_ref[...]), _rms(o_conv, cg_ref[...])], axis=1)
    x1 = x_ref[...] + jnp.dot(mixed.astype(BF16), wo_ref[...], preferred_element_type=F32)
    x1_ref[...] = x1
    xn = _rms(x1, fg_ref[...])
    xn_ref[...] = xn
    q_sc[...] = jnp.dot(xn.astype(BF16), wq_ref[...], preferred_element_type=F32)

    def head(h, carry):
        c0 = pl.multiple_of(h * PEER_DK, PEER_DK)
        q1 = q_sc[:, pl.ds(c0, D_HALF)].astype(BF16)
        q2 = q_sc[:, pl.ds(c0 + D_HALF, D_HALF)].astype(BF16)
        nt = (((1,), (1,)), ((), ()))
        s1 = lax.dot_general(keys_ref[0, h], q1, nt, preferred_element_type=F32)
        s2 = lax.dot_general(keys_ref[1, h], q2, nt, preferred_element_type=F32)
        v1, i1 = _topk_rows(s1, PEER_TOPK)
        v2, i2 = _topk_rows(s2, PEER_TOPK)
        cand = jnp.concatenate([v1[a:a + 1, :] + v2 for a in range(PEER_TOPK)], axis=0)
        best, pos = _topk_rows(cand, PEER_TOPK)
        a_sel = jnp.floor(pos * (1.0 / PEER_TOPK))
        b_sel = pos - a_sel * PEER_TOPK
        expert = _lookup_rows(a_sel, i1) * N_KEYS + _lookup_rows(b_sel, i2)
        e = jnp.exp(best - best[0:1, :])
        gate = e / jnp.sum(e, axis=0, keepdims=True)
        r0 = pl.multiple_of(h * PEER_TOPK, PEER_TOPK)
        it_sc[pl.ds(r0, PEER_TOPK), :] = expert
        gt_sc[pl.ds(r0, PEER_TOPK), :] = gate
        return carry

    lax.fori_loop(0, PEER_HEADS, head, 0)
    idx_ref[...] = it_sc[...].T.astype(jnp.int32)
    gt_ref[...] = gt_sc[...].T


def _mix_route(x2, oa, cv, conv_w, conv_b, attn_g, conv_g, wo_bf16, ffn_g, wq_bf16, keys_bf16, seq):
    m = x2.shape[0]
    tm = TM_MIX
    row = lambda i: (i, 0)
    const2 = lambda i: (0, 0)
    halo = tm // SUBLANES
    return pl.pallas_call(
        functools.partial(_mix_route_kernel, seq // tm),
        grid=(m // tm,),
        in_specs=[
            pl.BlockSpec((tm, D_MODEL), row),
            pl.BlockSpec((tm, SB_WIDTH), row),
            pl.BlockSpec((tm, CONV_WIDTH), lambda i: (i, 0)),
            pl.BlockSpec((tm, CONV_WIDTH), lambda i: (i, 1)),
            pl.BlockSpec((tm, CONV_WIDTH), lambda i: (i, 2)),
            pl.BlockSpec((SUBLANES, CONV_WIDTH), lambda i: (jnp.maximum(i * halo - 1, 0), 1)),
            pl.BlockSpec((SUBLANES, CONV_WIDTH), lambda i: (jnp.maximum(i * halo - 1, 0), 2)),
            pl.BlockSpec((CONV_K, CONV_WIDTH), const2),
            pl.BlockSpec((1, CONV_WIDTH), const2),
            pl.BlockSpec((1, SB_WIDTH), const2),
            pl.BlockSpec((1, CONV_WIDTH), const2),
            pl.BlockSpec((D_MODEL, D_MODEL), const2),
            pl.BlockSpec((1, D_MODEL), const2),
            pl.BlockSpec((D_MODEL, PEER_HEADS * PEER_DK), const2),
            pl.BlockSpec((2, PEER_HEADS, N_KEYS, D_HALF), lambda i: (0, 0, 0, 0)),
        ],
        out_specs=[
            pl.BlockSpec((tm, D_MODEL), row),
            pl.BlockSpec((tm, D_MODEL), row),
            pl.BlockSpec((tm, N_PICKS), row),
            pl.BlockSpec((tm, N_PICKS), row),
        ],
        out_shape=[
            jax.ShapeDtypeStruct((m, D_MODEL), F32),
            jax.ShapeDtypeStruct((m, D_MODEL), F32),
            jax.ShapeDtypeStruct((m, N_PICKS), jnp.int32),
            jax.ShapeDtypeStruct((m, N_PICKS), F32),
        ],
        scratch_shapes=[
            pltpu.VMEM((tm, PEER_HEADS * PEER_DK), F32),
            pltpu.VMEM((N_PICKS, tm), F32),
            pltpu.VMEM((N_PICKS, tm), F32),
        ],
        compiler_params=pltpu.CompilerParams(
            dimension_semantics=("parallel",), vmem_limit_bytes=VMEM_LIMIT),
        name="mix_route",
    )(x2, oa, cv, cv, cv, cv, cv, conv_w, conv_b.reshape(1, -1), attn_g.reshape(1, -1),
      conv_g.reshape(1, -1), wo_bf16, ffn_g.reshape(1, -1), wq_bf16, keys_bf16)


def _gelu_tanh(y):
    inner = math.sqrt(2.0 / math.pi) * (y + 0.044715 * (y * y * y))
    th = 1.0 - 2.0 / (jnp.exp(2.0 * inner) + 1.0)
    return 0.5 * y * (1.0 + th)


def _peer_sc_kernel(idx_hbm, g_hbm, x_hbm, u_hbm, v_hbm, out_hbm,
                    idx_v, g_v, x_v, out_v, ubuf, vbuf, red_v, w_v, usem, vsem):
    tok_per_w = x_hbm.shape[0] // SC_WORKERS
    n_groups = tok_per_w // SC_GROUP
    n_chunks = SC_GROUP * PEER_HEADS
    d_chunks = D_MODEL // SC_LANES
    wid = lax.axis_index("s") * SC_CORES + lax.axis_index("c")
    lane = lax.iota(jnp.int32, SC_LANES)
    zero = jnp.zeros((SC_LANES,), F32)

    def copies(j, b):
        return (pltpu.make_async_copy(u_hbm.at[idx_v.at[j]], ubuf.at[b], usem.at[b]),
                pltpu.make_async_copy(v_hbm.at[idx_v.at[j]], vbuf.at[b], vsem.at[b]))

    def compute(j, b):
        tl = j // PEER_HEADS

        def dot_body(c, accs):
            off = pl.multiple_of(c * SC_LANES, SC_LANES)
            xc = x_v[tl, pl.ds(off, SC_LANES)]
            return tuple(accs[k] + ubuf[b, k, pl.ds(off, SC_LANES)] * xc for k in range(PEER_TOPK))

        accs = lax.fori_loop(0, d_chunks, dot_body, tuple(zero for _ in range(PEER_TOPK)))
        for k in range(PEER_TOPK):
            red_v[k, pl.ds(SC_LANES, SC_LANES)] = accs[k]
        act = zero
        for l in range(SC_LANES):
            act = act + plsc.load_gather(red_v, [lane, jnp.full((SC_LANES,), SC_LANES + l, jnp.int32)])
        w_v[pl.ds(SC_LANES, SC_LANES)] = g_v[j, :] * _gelu_tanh(act)
        wb = [plsc.load_gather(w_v, [jnp.full((SC_LANES,), SC_LANES + k, jnp.int32)])
              for k in range(PEER_TOPK)]

        def acc_body(c, carry):
            off = pl.multiple_of(c * SC_LANES, SC_LANES)
            terms = [wb[k] * vbuf[b, k, pl.ds(off, SC_LANES)] for k in range(PEER_TOPK)]
            while len(terms) > 1:
                terms = [terms[p] + terms[p + 1] for p in range(0, len(terms), 2)]
            out_v[tl, pl.ds(off, SC_LANES)] = out_v[tl, pl.ds(off, SC_LANES)] + terms[0]
            return carry

        lax.fori_loop(0, d_chunks, acc_body, 0)

    def group(gi, carry):
        tok0 = pl.multiple_of(wid * tok_per_w + gi * SC_GROUP, SC_GROUP)
        row0 = pl.multiple_of(tok0 * PEER_HEADS, SC_GROUP * PEER_HEADS)
        pltpu.sync_copy(idx_hbm.at[pl.ds(row0, n_chunks)], idx_v)
        pltpu.sync_copy(g_hbm.at[pl.ds(row0, n_chunks)], g_v)
        pltpu.sync_copy(x_hbm.at[pl.ds(tok0, SC_GROUP)], x_v)

        def zero_body(c, carry2):
            off = pl.multiple_of(c * SC_LANES, SC_LANES)
            for t in range(SC_GROUP):
                out_v[t, pl.ds(off, SC_LANES)] = zero
            return carry2

        lax.fori_loop(0, d_chunks, zero_body, 0)
        for cp in copies(0, 0):
            cp.start()

        def pair(jj, carry2):
            for b in range(2):
                j = jj * 2 + b

                @pl.when(j + 1 < n_chunks)
                def _():
                    for cp in copies(j + 1, 1 - b):
                        cp.start()

                for cp in copies(j, b):
                    cp.wait()
                compute(j, b)
            return carry2

        lax.fori_loop(0, n_chunks // 2, pair, 0)
        pltpu.sync_copy(out_v, out_hbm.at[pl.ds(tok0, SC_GROUP)])
        return carry

    lax.fori_loop(0, n_groups, group, 0)


def _peer_experts(idx2, g2, xn, u, v):
    m = xn.shape[0]
    n_chunks = SC_GROUP * PEER_HEADS
    mesh = plsc.VectorSubcoreMesh(core_axis_name="c", subcore_axis_name="s")
    return pl.kernel(
        _peer_sc_kernel,
        out_type=jax.ShapeDtypeStruct((m, D_MODEL), F32),
        mesh=mesh,
        scratch_types=[
            pltpu.VMEM((n_chunks, PEER_TOPK), jnp.int32),
            pltpu.VMEM((n_chunks, PEER_TOPK), F32),
            pltpu.VMEM((SC_GROUP, D_MODEL), F32),
            pltpu.VMEM((SC_GROUP, D_MODEL), F32),
            pltpu.VMEM((2, PEER_TOPK, D_MODEL), F32),
            pltpu.VMEM((2, PEER_TOPK, D_MODEL), F32),
            pltpu.VMEM((PEER_TOPK, 2 * SC_LANES), F32),
            pltpu.VMEM((2 * SC_LANES,), F32),
            pltpu.SemaphoreType.DMA((2,)),
            pltpu.SemaphoreType.DMA((2,)),
        ],
        compiler_params=pltpu.CompilerParams(needs_layout_passes=False),
        name="peer_experts_sc",
    )(idx2, g2, xn, u, v)


def _final_kernel(x1_ref, p_ref, g_ref, o_ref):
    o_ref[...] = _rms(x1_ref[...] + p_ref[...], g_ref[...])


def _residual_norm(x1, p, g):
    m = x1.shape[0]
    row = lambda i: (i, 0)
    return pl.pallas_call(
        _final_kernel,
        grid=(m // TM_FINAL,),
        in_specs=[pl.BlockSpec((TM_FINAL, D_MODEL), row), pl.BlockSpec((TM_FINAL, D_MODEL), row),
                  pl.BlockSpec((1, D_MODEL), lambda i: (0, 0))],
        out_specs=pl.BlockSpec((TM_FINAL, D_MODEL), row),
        out_shape=jax.ShapeDtypeStruct((m, D_MODEL), F32),
        compiler_params=pltpu.CompilerParams(dimension_semantics=("parallel",)),
        name="residual_norm",
    )(x1, p, g.reshape(1, D_MODEL))


def kernel(x, mix_norm_g, w_in, conv_w, conv_b, attn_out_g, conv_out_g, w_out, ffn_norm_g,
           peer_wq, peer_keys, peer_u, peer_v, final_norm_g):
    b, s, d = x.shape
    assert w_in.shape[0] == 1, "single-layer block"
    w_in_b = w_in[0].astype(BF16)
    w_out_b = w_out[0].astype(BF16)
    wq_b = peer_wq[0].astype(BF16)
    keys_b = peer_keys[0].astype(BF16)
    bc = b // BATCH_CHUNKS
    mc = bc * s
    outs = []
    pending = None
    for c in range(BATCH_CHUNKS):
        x2 = x[c * bc:(c + 1) * bc].reshape(mc, d)
        qkv, cv = _inproj(x2, mix_norm_g[0], w_in_b)
        oa = _attention(qkv.reshape(bc, s, 3 * SB_WIDTH)).reshape(mc, SB_WIDTH)
        x1, xn, idx, gt = _mix_route(
            x2, oa, cv, conv_w[0], conv_b[0], attn_out_g[0], conv_out_g[0],
            w_out_b, ffn_norm_g[0], wq_b, keys_b, s)
        p = _peer_experts(idx.reshape(mc * PEER_HEADS, PEER_TOPK),
                          gt.reshape(mc * PEER_HEADS, PEER_TOPK), xn, peer_u[0], peer_v[0])
        if pending is not None:
            outs.append(_residual_norm(*pending, final_norm_g))
        pending = (x1, p)
    outs.append(_residual_norm(*pending, final_norm_g))
    return jnp.concatenate(outs, axis=0).reshape(b, s, d)
```

```python
import functools
import math

import jax
import jax.numpy as jnp
from jax import lax
from jax.experimental import pallas as pl
from jax.experimental.pallas import tpu as pltpu

F32 = jnp.float32
BF16 = jnp.bfloat16

D_MODEL = 1024
SB_HEADS = 8
HEAD_DIM = 64
SB_WIDTH = SB_HEADS * HEAD_DIM
CONV_WIDTH = D_MODEL - SB_WIDTH
CONV_K = 3
N_KEYS = 128
PEER_HEADS = 8
PEER_DK = 256
D_HALF = PEER_DK // 2
PEER_TOPK = 16
N_PICKS = PEER_HEADS * PEER_TOPK
EPS = 1e-6

SUBLANES = 8
LANES = 128
ROW_TILES = D_MODEL // LANES

TM_PROJ = 512
TQ = 256
TM_MIX = 256
TM_PEER = 512
TE_PEER = 1024
TM_FINAL = 512
VMEM_LIMIT = 56 * 1024 * 1024


def _rms(x, g):
    return x * lax.rsqrt(jnp.mean(x * x, axis=-1, keepdims=True) + EPS) * g


def _inproj_kernel(x_ref, g_ref, w_ref, qkv_ref, cv_ref):
    h = _rms(x_ref[...], g_ref[...])
    p = jnp.dot(h.astype(BF16), w_ref[...], preferred_element_type=F32)
    qkv_ref[...] = p[:, : 3 * SB_WIDTH].astype(BF16)
    cv_ref[...] = p[:, 3 * SB_WIDTH:]


def _inproj(x2, g, w_bf16):
    m = x2.shape[0]
    n = w_bf16.shape[1]
    return pl.pallas_call(
        _inproj_kernel,
        grid=(m // TM_PROJ,),
        in_specs=[
            pl.BlockSpec((TM_PROJ, D_MODEL), lambda i: (i, 0)),
            pl.BlockSpec((1, D_MODEL), lambda i: (0, 0)),
            pl.BlockSpec((D_MODEL, n), lambda i: (0, 0)),
        ],
        out_specs=[
            pl.BlockSpec((TM_PROJ, 3 * SB_WIDTH), lambda i: (i, 0)),
            pl.BlockSpec((TM_PROJ, 3 * CONV_WIDTH), lambda i: (i, 0)),
        ],
        out_shape=[
            jax.ShapeDtypeStruct((m, 3 * SB_WIDTH), BF16),
            jax.ShapeDtypeStruct((m, 3 * CONV_WIDTH), F32),
        ],
        compiler_params=pltpu.CompilerParams(
            dimension_semantics=("parallel",), vmem_limit_bytes=VMEM_LIMIT),
        name="inproj",
    )(x2, g.reshape(1, D_MODEL), w_bf16)


def _softplus(z):
    return jnp.maximum(z, 0.0) + jnp.log(1.0 + jnp.exp(-jnp.abs(z)))


def _attn_kernel(q_ref, k_ref, v_ref, o_ref):
    i = pl.program_id(2)
    scale = 1.0 / math.sqrt(HEAD_DIM)
    rows = lax.broadcasted_iota(jnp.int32, (TQ, TQ), 0)
    cols = lax.broadcasted_iota(jnp.int32, (TQ, TQ), 1)
    tri = (rows >= cols).astype(BF16)
    causal = rows > cols
    heads = [slice(hh * HEAD_DIM, (hh + 1) * HEAD_DIM) for hh in range(2)]
    qs = [q_ref[:, sl] * scale for sl in heads]

    def block(q, s0, sl, o, r, mask):
        k = k_ref[pl.ds(s0, TQ), sl]
        v = v_ref[pl.ds(s0, TQ), sl]
        z = lax.dot_general(q, k, (((1,), (1,)), ((), ())), preferred_element_type=F32)
        l1m = -_softplus(z)
        if mask is not None:
            l1m = jnp.where(mask, l1m, 0.0)
        l_hi = l1m.astype(BF16)
        l_lo = (l1m - l_hi.astype(F32)).astype(BF16)
        c = (jnp.dot(l_hi, tri, preferred_element_type=F32)
             + jnp.dot(l_lo, tri, preferred_element_type=F32) + r)
        a = jnp.exp(z + c)
        if mask is not None:
            a = jnp.where(mask, a, 0.0)
        o = o + jnp.dot(a.astype(BF16), v, preferred_element_type=F32)
        return o, c[:, 0:1]

    zero_o = jnp.zeros((TQ, HEAD_DIM), F32)
    zero_r = jnp.zeros((TQ, 1), F32)
    diag0 = pl.multiple_of(i * TQ, TQ)
    carry = tuple(block(qs[hh], diag0, heads[hh], zero_o, zero_r, causal) for hh in range(2))

    def body(jj, carry):
        s0 = pl.multiple_of((i - jj) * TQ, TQ)
        return tuple(block(qs[hh], s0, heads[hh], carry[hh][0], carry[hh][1], None)
                     for hh in range(2))

    carry = lax.fori_loop(1, i + 1, body, carry)
    o_ref[...] = jnp.concatenate([carry[0][0], carry[1][0]], axis=1)


def _attention(qkv3):
    b, s, _ = qkv3.shape
    hp = SB_HEADS // 2
    return pl.pallas_call(
        _attn_kernel,
        grid=(b, hp, s // TQ),
        in_specs=[
            pl.BlockSpec((None, TQ, LANES), lambda bi, h, i: (bi, i, h)),
            pl.BlockSpec((None, s, LANES), lambda bi, h, i: (bi, 0, hp + h)),
            pl.BlockSpec((None, s, LANES), lambda bi, h, i: (bi, 0, 2 * hp + h)),
        ],
        out_specs=pl.BlockSpec((None, TQ, LANES), lambda bi, h, i: (bi, i, h)),
        out_shape=jax.ShapeDtypeStruct((b, s, SB_WIDTH), F32),
        compiler_params=pltpu.CompilerParams(
            dimension_semantics=("parallel", "parallel", "arbitrary"),
            vmem_limit_bytes=VMEM_LIMIT),
        name="sb_attention",
    )(qkv3, qkv3, qkv3)


def _topk_rows(s, k):
    n = s.shape[0]
    iota = lax.broadcasted_iota(jnp.int32, s.shape, 0).astype(F32)
    rank = jnp.full(s.shape, float(k), F32)
    vals, idxs = [], []
    for r in range(k):
        m = jnp.max(s, axis=0, keepdims=True)
        im = jnp.min(jnp.where(s == m, iota, float(n)), axis=0, keepdims=True)
        hit = iota == im
        vals.append(m)
        idxs.append(im)
        rank = jnp.where(hit, float(r), rank)
        s = jnp.where(hit, -jnp.inf, s)
    return jnp.concatenate(vals, axis=0), jnp.concatenate(idxs, axis=0), rank


def _mix_route_kernel(tiles_per_seq, x_ref, oa_ref, gb_ref, gc_ref, hc_ref, gcp_ref, hcp_ref,
                      cw_ref, cb_ref, ag_ref, cg_ref, wo_ref, fg_ref, wq_ref, keys_ref,
                      x1_ref, xnt_ref, cnt_ref, e1_ref, rk2_ref, e2_ref, q_sc):
    i = pl.program_id(0)
    tm = x_ref.shape[0]
    u = gc_ref[...] * hc_ref[...]
    first = (i % tiles_per_seq) == 0
    u_prev = jnp.where(first, 0.0, gcp_ref[...] * hcp_ref[...])
    u_ext = jnp.concatenate([u_prev, u], axis=0)
    cw = cw_ref[...]
    y = (cw[0:1, :] * u_ext[SUBLANES - 2:SUBLANES - 2 + tm, :]
         + cw[1:2, :] * u_ext[SUBLANES - 1:SUBLANES - 1 + tm, :]
         + cw[2:3, :] * u) + cb_ref[...]
    o_conv = gb_ref[...] * y
    mixed = jnp.concatenate([_rms(oa_ref[...], ag_ref[...]), _rms(o_conv, cg_ref[...])], axis=1)
    x1 = x_ref[...] + jnp.dot(mixed.astype(BF16), wo_ref[...], preferred_element_type=F32)
    x1_ref[...] = x1
    xn = _rms(x1, fg_ref[...]).astype(BF16)
    xnt_ref[...] = xn.T
    q_sc[...] = jnp.dot(xn, wq_ref[...], preferred_element_type=F32)

    def head(h, carry):
        c0 = pl.multiple_of(h * PEER_DK, PEER_DK)
        q1 = q_sc[:, pl.ds(c0, D_HALF)].astype(BF16)
        q2 = q_sc[:, pl.ds(c0 + D_HALF, D_HALF)].astype(BF16)
        nt = (((1,), (1,)), ((), ()))
        s1 = lax.dot_general(keys_ref[0, h], q1, nt, preferred_element_type=F32)
        s2 = lax.dot_general(keys_ref[1, h], q2, nt, preferred_element_type=F32)
        v1, _, rk1 = _topk_rows(s1, PEER_TOPK)
        v2, _, rk2 = _topk_rows(s2, PEER_TOPK)
        cand = jnp.concatenate([v1[a:a + 1, :] + v2 for a in range(PEER_TOPK)], axis=0)
        best, pos, _ = _topk_rows(cand, PEER_TOPK)
        a_sel = jnp.floor(pos * (1.0 / PEER_TOPK))
        cnt = jnp.zeros_like(s1)
        for a in range(PEER_TOPK):
            n_a = jnp.sum((a_sel == float(a)).astype(F32), axis=0, keepdims=True)
            cnt = cnt + jnp.where(rk1 == float(a), n_a, 0.0)
        z = jnp.sum(jnp.exp(best - best[0:1, :]), axis=0, keepdims=True)
        cnt_ref[h] = cnt
        e1_ref[h] = jnp.exp(s1 - v1[0:1, :])
        rk2_ref[h] = rk2
        e2_ref[h] = jnp.exp(s2 - v2[0:1, :]) / z
        return carry

    lax.fori_loop(0, PEER_HEADS, head, 0)


def _mix_route(x2, oa, cv, conv_w, conv_b, attn_g, conv_g, wo_bf16, ffn_g, wq_bf16, keys_bf16, seq):
    m = x2.shape[0]
    tm = TM_MIX
    row = lambda i: (i, 0)
    const2 = lambda i: (0, 0)
    halo = tm // SUBLANES
    return pl.pallas_call(
        functools.partial(_mix_route_kernel, seq // tm),
        grid=(m // tm,),
        in_specs=[
            pl.BlockSpec((tm, D_MODEL), row),
            pl.BlockSpec((tm, SB_WIDTH), row),
            pl.BlockSpec((tm, CONV_WIDTH), lambda i: (i, 0)),
            pl.BlockSpec((tm, CONV_WIDTH), lambda i: (i, 1)),
            pl.BlockSpec((tm, CONV_WIDTH), lambda i: (i, 2)),
            pl.BlockSpec((SUBLANES, CONV_WIDTH), lambda i: (jnp.maximum(i * halo - 1, 0), 1)),
            pl.BlockSpec((SUBLANES, CONV_WIDTH), lambda i: (jnp.maximum(i * halo - 1, 0), 2)),
            pl.BlockSpec((CONV_K, CONV_WIDTH), const2),
            pl.BlockSpec((1, CONV_WIDTH), const2),
            pl.BlockSpec((1, SB_WIDTH), const2),
            pl.BlockSpec((1, CONV_WIDTH), const2),
            pl.BlockSpec((D_MODEL, D_MODEL), const2),
            pl.BlockSpec((1, D_MODEL), const2),
            pl.BlockSpec((D_MODEL, PEER_HEADS * PEER_DK), const2),
            pl.BlockSpec((2, PEER_HEADS, N_KEYS, D_HALF), lambda i: (0, 0, 0, 0)),
        ],
        out_specs=[
            pl.BlockSpec((tm, D_MODEL), row),
            pl.BlockSpec((D_MODEL, tm), lambda i: (0, i)),
        ] + [pl.BlockSpec((PEER_HEADS, N_KEYS, tm), lambda i: (0, 0, i))] * 4,
        out_shape=[
            jax.ShapeDtypeStruct((m, D_MODEL), F32),
            jax.ShapeDtypeStruct((D_MODEL, m), BF16),
        ] + [jax.ShapeDtypeStruct((PEER_HEADS, N_KEYS, m), F32)] * 4,
        scratch_shapes=[pltpu.VMEM((tm, PEER_HEADS * PEER_DK), F32)],
        compiler_params=pltpu.CompilerParams(
            dimension_semantics=("parallel",), vmem_limit_bytes=VMEM_LIMIT),
        name="mix_route",
    )(x2, oa, cv, cv, cv, cv, cv, conv_w, conv_b.reshape(1, -1), attn_g.reshape(1, -1),
      conv_g.reshape(1, -1), wo_bf16, ffn_g.reshape(1, -1), wq_bf16, keys_bf16)


def _peer_kernel(xnt_ref, u_ref, vt_ref, cnt_ref, e1_ref, rk2_ref, e2_ref, out_ref, acc_sc):
    e = pl.program_id(1)

    @pl.when(e == 0)
    def _():
        acc_sc[...] = jnp.zeros_like(acc_sc)

    xnt = xnt_ref[...]
    rows = TE_PEER // N_KEYS
    w_parts = []
    for r in range(rows):
        sl = slice(r * N_KEYS, (r + 1) * N_KEYS)
        act = jnp.dot(u_ref[sl, :], xnt, preferred_element_type=F32)
        gate = jnp.zeros_like(act)
        for h in range(PEER_HEADS):
            picked = rk2_ref[h] < cnt_ref[h, r:r + 1, :]
            gate = gate + jnp.where(picked, e2_ref[h] * e1_ref[h, r:r + 1, :], 0.0)
        w_parts.append((jax.nn.gelu(act) * gate).astype(BF16))
    w = jnp.concatenate(w_parts, axis=0)
    acc_sc[...] += jnp.dot(vt_ref[...], w, preferred_element_type=F32)

    @pl.when(e == pl.num_programs(1) - 1)
    def _():
        out_ref[...] = acc_sc[...].T


def _peer_experts(xnt, u_bf16, vt_bf16, cnt, e1, rk2, e2):
    m = xnt.shape[1]
    n_exp = u_bf16.shape[0]
    tm, te = TM_PEER, TE_PEER
    rows = te // N_KEYS
    per_row = pl.BlockSpec((PEER_HEADS, rows, tm), lambda t, e: (0, e, t))
    per_key = pl.BlockSpec((PEER_HEADS, N_KEYS, tm), lambda t, e: (0, 0, t))
    return pl.pallas_call(
        _peer_kernel,
        grid=(m // tm, n_exp // te),
        in_specs=[
            pl.BlockSpec((D_MODEL, tm), lambda t, e: (0, t)),
            pl.BlockSpec((te, D_MODEL), lambda t, e: (e, 0)),
            pl.BlockSpec((D_MODEL, te), lambda t, e: (0, e)),
            per_row, per_row, per_key, per_key,
        ],
        out_specs=pl.BlockSpec((tm, D_MODEL), lambda t, e: (t, 0)),
        out_shape=jax.ShapeDtypeStruct((m, D_MODEL), F32),
        scratch_shapes=[pltpu.VMEM((D_MODEL, tm), F32)],
        compiler_params=pltpu.CompilerParams(
            dimension_semantics=("parallel", "arbitrary"), vmem_limit_bytes=VMEM_LIMIT),
        name="peer_experts",
    )(xnt, u_bf16, vt_bf16, cnt, e1, rk2, e2)


def _final_kernel(x1_ref, p_ref, g_ref, o_ref):
    o_ref[...] = _rms(x1_ref[...] + p_ref[...], g_ref[...])


def _residual_norm(x1, p, g):
    m = x1.shape[0]
    row = lambda i: (i, 0)
    return pl.pallas_call(
        _final_kernel,
        grid=(m // TM_FINAL,),
        in_specs=[pl.BlockSpec((TM_FINAL, D_MODEL), row), pl.BlockSpec((TM_FINAL, D_MODEL), row),
                  pl.BlockSpec((1, D_MODEL), lambda i: (0, 0))],
        out_specs=pl.BlockSpec((TM_FINAL, D_MODEL), row),
        out_shape=jax.ShapeDtypeStruct((m, D_MODEL), F32),
        compiler_params=pltpu.CompilerParams(dimension_semantics=("parallel",)),
        name="residual_norm",
    )(x1, p, g.reshape(1, D_MODEL))


def kernel(x, mix_norm_g, w_in, conv_w, conv_b, attn_out_g, conv_out_g, w_out, ffn_norm_g,
           peer_wq, peer_keys, peer_u, peer_v, final_norm_g):
    b, s, d = x.shape
    assert w_in.shape[0] == 1, "single-layer block"
    w_in_b = w_in[0].astype(BF16)
    w_out_b = w_out[0].astype(BF16)
    wq_b = peer_wq[0].astype(BF16)
    keys_b = peer_keys[0].astype(BF16)
    u_b = peer_u[0].astype(BF16)
    vt_b = peer_v[0].T.astype(BF16)
    m = b * s
    x2 = x.reshape(m, d)
    qkv, cv = _inproj(x2, mix_norm_g[0], w_in_b)
    oa = _attention(qkv.reshape(b, s, 3 * SB_WIDTH)).reshape(m, SB_WIDTH)
    x1, xnt, cnt, e1, rk2, e2 = _mix_route(
        x2, oa, cv, conv_w[0], conv_b[0], attn_out_g[0], conv_out_g[0],
        w_out_b, ffn_norm_g[0], wq_b, keys_b, s)
    p = _peer_experts(xnt, u_b, vt_b, cnt, e1, rk2, e2)
    return _residual_norm(x1, p, final_norm_g).reshape(b, s, d)
```

```python
import functools
import math

import jax
import jax.numpy as jnp
from jax import lax
from jax.experimental import pallas as pl
from jax.experimental.pallas import tpu as pltpu

F32 = jnp.float32
BF16 = jnp.bfloat16

D_MODEL = 1024
SB_HEADS = 8
HEAD_DIM = 64
SB_WIDTH = SB_HEADS * HEAD_DIM
CONV_WIDTH = D_MODEL - SB_WIDTH
CONV_K = 3
N_KEYS = 128
PEER_HEADS = 8
PEER_DK = 256
D_HALF = PEER_DK // 2
PEER_TOPK = 16
N_PICKS = PEER_HEADS * PEER_TOPK
EPS = 1e-6

SUBLANES = 8
LANES = 128
ROW_TILES = D_MODEL // LANES

TM_PROJ = 512
TQ = 256
TM_MIX = 256
TM_PEER = 512
TE_PEER = 1024
TM_FINAL = 512
VMEM_LIMIT = 56 * 1024 * 1024


def _rms(x, g):
    return x * lax.rsqrt(jnp.mean(x * x, axis=-1, keepdims=True) + EPS) * g


def _inproj_kernel(x_ref, g_ref, w_ref, qkv_ref, cv_ref):
    h = _rms(x_ref[...], g_ref[...])
    p = jnp.dot(h.astype(BF16), w_ref[...], preferred_element_type=F32)
    qkv_ref[...] = p[:, : 3 * SB_WIDTH].astype(BF16)
    cv_ref[...] = p[:, 3 * SB_WIDTH:]


def _inproj(x2, g, w_bf16):
    m = x2.shape[0]
    n = w_bf16.shape[1]
    return pl.pallas_call(
        _inproj_kernel,
        grid=(m // TM_PROJ,),
        in_specs=[
            pl.BlockSpec((TM_PROJ, D_MODEL), lambda i: (i, 0)),
            pl.BlockSpec((1, D_MODEL), lambda i: (0, 0)),
            pl.BlockSpec((D_MODEL, n), lambda i: (0, 0)),
        ],
        out_specs=[
            pl.BlockSpec((TM_PROJ, 3 * SB_WIDTH), lambda i: (i, 0)),
            pl.BlockSpec((TM_PROJ, 3 * CONV_WIDTH), lambda i: (i, 0)),
        ],
        out_shape=[
            jax.ShapeDtypeStruct((m, 3 * SB_WIDTH), BF16),
            jax.ShapeDtypeStruct((m, 3 * CONV_WIDTH), F32),
        ],
        compiler_params=pltpu.CompilerParams(
            dimension_semantics=("parallel",), vmem_limit_bytes=VMEM_LIMIT),
        name="inproj",
    )(x2, g.reshape(1, D_MODEL), w_bf16)


def _softplus(z):
    return jnp.maximum(z, 0.0) + jnp.log(1.0 + jnp.exp(-jnp.abs(z)))


def _attn_kernel(q_ref, k_ref, v_ref, o_ref):
    i = pl.program_id(2)
    scale = 1.0 / math.sqrt(HEAD_DIM)
    rows = lax.broadcasted_iota(jnp.int32, (TQ, TQ), 0)
    cols = lax.broadcasted_iota(jnp.int32, (TQ, TQ), 1)
    tri = (rows >= cols).astype(BF16)
    causal = rows > cols
    heads = [slice(hh * HEAD_DIM, (hh + 1) * HEAD_DIM) for hh in range(2)]
    qs = [q_ref[:, sl] * scale for sl in heads]

    def block(q, s0, sl, o, r, mask):
        k = k_ref[pl.ds(s0, TQ), sl]
        v = v_ref[pl.ds(s0, TQ), sl]
        z = lax.dot_general(q, k, (((1,), (1,)), ((), ())), preferred_element_type=F32)
        l1m = -_softplus(z)
        if mask is not None:
            l1m = jnp.where(mask, l1m, 0.0)
        l_hi = l1m.astype(BF16)
        l_lo = (l1m - l_hi.astype(F32)).astype(BF16)
        c = (jnp.dot(l_hi, tri, preferred_element_type=F32)
             + jnp.dot(l_lo, tri, preferred_element_type=F32) + r)
        a = jnp.exp(z + c)
        if mask is not None:
            a = jnp.where(mask, a, 0.0)
        o = o + jnp.dot(a.astype(BF16), v, preferred_element_type=F32)
        return o, c[:, 0:1]

    zero_o = jnp.zeros((TQ, HEAD_DIM), F32)
    zero_r = jnp.zeros((TQ, 1), F32)
    diag0 = pl.multiple_of(i * TQ, TQ)
    carry = tuple(block(qs[hh], diag0, heads[hh], zero_o, zero_r, causal) for hh in range(2))

    def body(jj, carry):
        s0 = pl.multiple_of((i - jj) * TQ, TQ)
        return tuple(block(qs[hh], s0, heads[hh], carry[hh][0], carry[hh][1], None)
                     for hh in range(2))

    carry = lax.fori_loop(1, i + 1, body, carry)
    o_ref[...] = jnp.concatenate([carry[0][0], carry[1][0]], axis=1)


def _attention(qkv3):
    b, s, _ = qkv3.shape
    hp = SB_HEADS // 2
    return pl.pallas_call(
        _attn_kernel,
        grid=(b, hp, s // TQ),
        in_specs=[
            pl.BlockSpec((None, TQ, LANES), lambda bi, h, i: (bi, i, h)),
            pl.BlockSpec((None, s, LANES), lambda bi, h, i: (bi, 0, hp + h)),
            pl.BlockSpec((None, s, LANES), lambda bi, h, i: (bi, 0, 2 * hp + h)),
        ],
        out_specs=pl.BlockSpec((None, TQ, LANES), lambda bi, h, i: (bi, i, h)),
        out_shape=jax.ShapeDtypeStruct((b, s, SB_WIDTH), F32),
        compiler_params=pltpu.CompilerParams(
            dimension_semantics=("parallel", "parallel", "arbitrary"),
            vmem_limit_bytes=VMEM_LIMIT),
        name="sb_attention",
    )(qkv3, qkv3, qkv3)


def _topk_rows(s, k):
    n = s.shape[0]
    iota = lax.broadcasted_iota(jnp.int32, s.shape, 0).astype(F32)
    rank = jnp.full(s.shape, float(k), F32)
    vals, idxs = [], []
    for r in range(k):
        m = jnp.max(s, axis=0, keepdims=True)
        im = jnp.min(jnp.where(s == m, iota, float(n)), axis=0, keepdims=True)
        hit = iota == im
        vals.append(m)
        idxs.append(im)
        rank = jnp.where(hit, float(r), rank)
        s = jnp.where(hit, -jnp.inf, s)
    return jnp.concatenate(vals, axis=0), jnp.concatenate(idxs, axis=0), rank


def _mix_route_kernel(tiles_per_seq, x_ref, oa_ref, gb_ref, gc_ref, hc_ref, gcp_ref, hcp_ref,
                      cw_ref, cb_ref, ag_ref, cg_ref, wo_ref, fg_ref, wq_ref, keys_ref,
                      x1_ref, xnt_ref, cnt_ref, e1_ref, rk2_ref, e2_ref, q_sc):
    i = pl.program_id(0)
    tm = x_ref.shape[0]
    u = gc_ref[...] * hc_ref[...]
    first = (i % tiles_per_seq) == 0
    u_prev = jnp.where(first, 0.0, gcp_ref[...] * hcp_ref[...])
    u_ext = jnp.concatenate([u_prev, u], axis=0)
    cw = cw_ref[...]
    y = (cw[0:1, :] * u_ext[SUBLANES - 2:SUBLANES - 2 + tm, :]
         + cw[1:2, :] * u_ext[SUBLANES - 1:SUBLANES - 1 + tm, :]
         + cw[2:3, :] * u) + cb_ref[...]
    o_conv = gb_ref[...] * y
    mixed = jnp.concatenate([_rms(oa_ref[...], ag_ref[...]), _rms(o_conv, cg_ref[...])], axis=1)
    x1 = x_ref[...] + jnp.dot(mixed.astype(BF16), wo_ref[...], preferred_element_type=F32)
    x1_ref[...] = x1
    xn = _rms(x1, fg_ref[...]).astype(BF16)
    xnt_ref[...] = xn.T
    q_sc[...] = jnp.dot(xn, wq_ref[...], preferred_element_type=F32)

    def head(h, carry):
        c0 = pl.multiple_of(h * PEER_DK, PEER_DK)
        q1 = q_sc[:, pl.ds(c0, D_HALF)].astype(BF16)
        q2 = q_sc[:, pl.ds(c0 + D_HALF, D_HALF)].astype(BF16)
        nt = (((1,), (1,)), ((), ()))
        s1 = lax.dot_general(keys_ref[0, h], q1, nt, preferred_element_type=F32)
        s2 = lax.dot_general(keys_ref[1, h], q2, nt, preferred_element_type=F32)
        v1, _, rk1 = _topk_rows(s1, PEER_TOPK)
        v2, _, rk2 = _topk_rows(s2, PEER_TOPK)
        widths = [PEER_TOPK // (a + 1) for a in range(PEER_TOPK)]
        n_cand = sum(widths)
        pad = -n_cand % SUBLANES
        cand = jnp.concatenate(
            [v1[a:a + 1, :] + v2[0:widths[a], :] for a in range(PEER_TOPK)]
            + [jnp.full((pad, tm), -jnp.inf, F32)], axis=0)
        iota = lax.broadcasted_iota(jnp.int32, cand.shape, 0).astype(F32)
        left = cand
        z = jnp.zeros((1, tm), F32)
        for r in range(PEER_TOPK):
            m = jnp.max(left, axis=0, keepdims=True)
            if r == 0:
                m0 = m
            im = jnp.min(jnp.where(left == m, iota, float(n_cand + pad)), axis=0, keepdims=True)
            left = jnp.where(iota == im, -jnp.inf, left)
            z = z + jnp.exp(m - m0)
        picked = jnp.where(left == cand, 0.0, 1.0)
        cnt = jnp.zeros_like(s1)
        off = 0
        for a in range(PEER_TOPK):
            n_a = jnp.sum(picked[off:off + widths[a], :], axis=0, keepdims=True)
            cnt = cnt + jnp.where(rk1 == float(a), n_a, 0.0)
            off += widths[a]
        cnt_ref[h] = cnt
        e1_ref[h] = jnp.exp(s1 - v1[0:1, :])
        rk2_ref[h] = rk2.astype(BF16)
        e2_ref[h] = (jnp.exp(s2 - v2[0:1, :]) / z).astype(BF16)
        return carry

    lax.fori_loop(0, PEER_HEADS, head, 0)


def _mix_route(x2, oa, cv, conv_w, conv_b, attn_g, conv_g, wo_bf16, ffn_g, wq_bf16, keys_bf16, seq):
    m = x2.shape[0]
    tm = TM_MIX
    row = lambda i: (i, 0)
    const2 = lambda i: (0, 0)
    halo = tm // SUBLANES
    return pl.pallas_call(
        functools.partial(_mix_route_kernel, seq // tm),
        grid=(m // tm,),
        in_specs=[
            pl.BlockSpec((tm, D_MODEL), row),
            pl.BlockSpec((tm, SB_WIDTH), row),
            pl.BlockSpec((tm, CONV_WIDTH), lambda i: (i, 0)),
            pl.BlockSpec((tm, CONV_WIDTH), lambda i: (i, 1)),
            pl.BlockSpec((tm, CONV_WIDTH), lambda i: (i, 2)),
            pl.BlockSpec((SUBLANES, CONV_WIDTH), lambda i: (jnp.maximum(i * halo - 1, 0), 1)),
            pl.BlockSpec((SUBLANES, CONV_WIDTH), lambda i: (jnp.maximum(i * halo - 1, 0), 2)),
            pl.BlockSpec((CONV_K, CONV_WIDTH), const2),
            pl.BlockSpec((1, CONV_WIDTH), const2),
            pl.BlockSpec((1, SB_WIDTH), const2),
            pl.BlockSpec((1, CONV_WIDTH), const2),
            pl.BlockSpec((D_MODEL, D_MODEL), const2),
            pl.BlockSpec((1, D_MODEL), const2),
            pl.BlockSpec((D_MODEL, PEER_HEADS * PEER_DK), const2),
            pl.BlockSpec((2, PEER_HEADS, N_KEYS, D_HALF), lambda i: (0, 0, 0, 0)),
        ],
        out_specs=[
            pl.BlockSpec((tm, D_MODEL), row),
            pl.BlockSpec((D_MODEL, tm), lambda i: (0, i)),
        ] + [pl.BlockSpec((PEER_HEADS, N_KEYS, tm), lambda i: (0, 0, i))] * 4,
        out_shape=[
            jax.ShapeDtypeStruct((m, D_MODEL), F32),
            jax.ShapeDtypeStruct((D_MODEL, m), BF16),
        ] + [jax.ShapeDtypeStruct((PEER_HEADS, N_KEYS, m), dt) for dt in (F32, F32, BF16, BF16)],
        scratch_shapes=[pltpu.VMEM((tm, PEER_HEADS * PEER_DK), F32)],
        compiler_params=pltpu.CompilerParams(
            dimension_semantics=("parallel",), vmem_limit_bytes=VMEM_LIMIT),
        name="mix_route",
    )(x2, oa, cv, cv, cv, cv, cv, conv_w, conv_b.reshape(1, -1), attn_g.reshape(1, -1),
      conv_g.reshape(1, -1), wo_bf16, ffn_g.reshape(1, -1), wq_bf16, keys_bf16)


def _peer_kernel(xnt_ref, u_ref, vt_ref, cnt_ref, e1_ref, rk2_ref, e2_ref, out_ref, acc_sc):
    e = pl.program_id(1)

    @pl.when(e == 0)
    def _():
        acc_sc[...] = jnp.zeros_like(acc_sc)

    xnt = xnt_ref[...]
    rows = TE_PEER // N_KEYS
    w_parts = []
    for r in range(rows):
        sl = slice(r * N_KEYS, (r + 1) * N_KEYS)
        act = jnp.dot(u_ref[sl, :], xnt, preferred_element_type=F32)
        gate = jnp.zeros(act.shape, BF16)
        for h in range(PEER_HEADS):
            picked = rk2_ref[h] < cnt_ref[h, r:r + 1, :].astype(BF16)
            gate = gate + jnp.where(picked, e2_ref[h] * e1_ref[h, r:r + 1, :].astype(BF16), 0.0)
        w_parts.append(jax.nn.gelu(act).astype(BF16) * gate)
    w = jnp.concatenate(w_parts, axis=0)
    acc_sc[...] += jnp.dot(vt_ref[...], w, preferred_element_type=F32)

    @pl.when(e == pl.num_programs(1) - 1)
    def _():
        out_ref[...] = acc_sc[...].T


def _peer_experts(xnt, u_bf16, vt_bf16, cnt, e1, rk2, e2):
    m = xnt.shape[1]
    n_exp = u_bf16.shape[0]
    tm, te = TM_PEER, TE_PEER
    rows = te // N_KEYS
    per_row = pl.BlockSpec((PEER_HEADS, rows, tm), lambda t, e: (0, e, t))
    per_key = pl.BlockSpec((PEER_HEADS, N_KEYS, tm), lambda t, e: (0, 0, t))
    return pl.pallas_call(
        _peer_kernel,
        grid=(m // tm, n_exp // te),
        in_specs=[
            pl.BlockSpec((D_MODEL, tm), lambda t, e: (0, t)),
            pl.BlockSpec((te, D_MODEL), lambda t, e: (e, 0)),
            pl.BlockSpec((D_MODEL, te), lambda t, e: (0, e)),
            per_row, per_row, per_key, per_key,
        ],
        out_specs=pl.BlockSpec((tm, D_MODEL), lambda t, e: (t, 0)),
        out_shape=jax.ShapeDtypeStruct((m, D_MODEL), F32),
        scratch_shapes=[pltpu.VMEM((D_MODEL, tm), F32)],
        compiler_params=pltpu.CompilerParams(
            dimension_semantics=("parallel", "arbitrary"), vmem_limit_bytes=VMEM_LIMIT),
        name="peer_experts",
    )(xnt, u_bf16, vt_bf16, cnt, e1, rk2, e2)


def _final_kernel(x1_ref, p_ref, g_ref, o_ref):
    o_ref[...] = _rms(x1_ref[...] + p_ref[...], g_ref[...])


def _residual_norm(x1, p, g):
    m = x1.shape[0]
    row = lambda i: (i, 0)
    return pl.pallas_call(
        _final_kernel,
        grid=(m // TM_FINAL,),
        in_specs=[pl.BlockSpec((TM_FINAL, D_MODEL), row), pl.BlockSpec((TM_FINAL, D_MODEL), row),
                  pl.BlockSpec((1, D_MODEL), lambda i: (0, 0))],
        out_specs=pl.BlockSpec((TM_FINAL, D_MODEL), row),
        out_shape=jax.ShapeDtypeStruct((m, D_MODEL), F32),
        compiler_params=pltpu.CompilerParams(dimension_semantics=("parallel",)),
        name="residual_norm",
    )(x1, p, g.reshape(1, D_MODEL))


def kernel(x, mix_norm_g, w_in, conv_w, conv_b, attn_out_g, conv_out_g, w_out, ffn_norm_g,
           peer_wq, peer_keys, peer_u, peer_v, final_norm_g):
    b, s, d = x.shape
    assert w_in.shape[0] == 1, "single-layer block"
    w_in_b = w_in[0].astype(BF16)
    w_out_b = w_out[0].astype(BF16)
    wq_b = peer_wq[0].astype(BF16)
    keys_b = peer_keys[0].astype(BF16)
    u_b = peer_u[0].astype(BF16)
    vt_b = peer_v[0].T.astype(BF16)
    m = b * s
    x2 = x.reshape(m, d)
    qkv, cv = _inproj(x2, mix_norm_g[0], w_in_b)
    oa = _attention(qkv.reshape(b, s, 3 * SB_WIDTH)).reshape(m, SB_WIDTH)
    x1, xnt, cnt, e1, rk2, e2 = _mix_route(
        x2, oa, cv, conv_w[0], conv_b[0], attn_out_g[0], conv_out_g[0],
        w_out_b, ffn_norm_g[0], wq_b, keys_b, s)
    p = _peer_experts(xnt, u_b, vt_b, cnt, e1, rk2, e2)
    return _residual_norm(x1, p, final_norm_g).reshape(b, s, d)
```

```python
import functools
import math

import jax
import jax.numpy as jnp
from jax import lax
from jax.experimental import pallas as pl
from jax.experimental.pallas import tpu as pltpu

F32 = jnp.float32
BF16 = jnp.bfloat16

D_MODEL = 1024
SB_HEADS = 8
HEAD_DIM = 64
SB_WIDTH = SB_HEADS * HEAD_DIM
CONV_WIDTH = D_MODEL - SB_WIDTH
CONV_K = 3
N_KEYS = 128
PEER_HEADS = 8
PEER_DK = 256
D_HALF = PEER_DK // 2
PEER_TOPK = 16
N_PICKS = PEER_HEADS * PEER_TOPK
EPS = 1e-6
GELU_C0 = math.sqrt(2.0 / math.pi)
GELU_C1 = 0.044715

SUBLANES = 8
LANES = 128
ROW_TILES = D_MODEL // LANES

TM_PROJ = 512
TQ = 512
TK = 256
TM_MIX = 256
TM_PEER = 512
TE_PEER = 1024
TM_FINAL = 512
VMEM_LIMIT = 56 * 1024 * 1024


def _rms(x, g):
    return x * lax.rsqrt(jnp.mean(x * x, axis=-1, keepdims=True) + EPS) * g


def _inproj_kernel(x_ref, g_ref, w_ref, qkv_ref, cv_ref):
    h = _rms(x_ref[...], g_ref[...])
    p = jnp.dot(h.astype(BF16), w_ref[...], preferred_element_type=F32)
    qkv_ref[...] = p[:, : 3 * SB_WIDTH].astype(BF16)
    cv_ref[...] = p[:, 3 * SB_WIDTH:]


def _inproj(x2, g, w_bf16):
    m = x2.shape[0]
    n = w_bf16.shape[1]
    return pl.pallas_call(
        _inproj_kernel,
        grid=(m // TM_PROJ,),
        in_specs=[
            pl.BlockSpec((TM_PROJ, D_MODEL), lambda i: (i, 0)),
            pl.BlockSpec((1, D_MODEL), lambda i: (0, 0)),
            pl.BlockSpec((D_MODEL, n), lambda i: (0, 0)),
        ],
        out_specs=[
            pl.BlockSpec((TM_PROJ, 3 * SB_WIDTH), lambda i: (i, 0)),
            pl.BlockSpec((TM_PROJ, 3 * CONV_WIDTH), lambda i: (i, 0)),
        ],
        out_shape=[
            jax.ShapeDtypeStruct((m, 3 * SB_WIDTH), BF16),
            jax.ShapeDtypeStruct((m, 3 * CONV_WIDTH), F32),
        ],
        compiler_params=pltpu.CompilerParams(
            dimension_semantics=("parallel",), vmem_limit_bytes=VMEM_LIMIT),
        name="inproj",
    )(x2, g.reshape(1, D_MODEL), w_bf16)


def _softplus(z):
    return jnp.maximum(z, 0.0) + jnp.log(1.0 + jnp.exp(-jnp.abs(z)))


def _attn_kernel(q_ref, k_ref, v_ref, o_ref):
    i = pl.program_id(2)
    scale = 1.0 / math.sqrt(HEAD_DIM)
    kr = lax.broadcasted_iota(jnp.int32, (TK, TK), 0)
    kc = lax.broadcasted_iota(jnp.int32, (TK, TK), 1)
    tri = (kr >= kc).astype(BF16)
    rows = lax.broadcasted_iota(jnp.int32, (TQ, TK), 0)
    cols = lax.broadcasted_iota(jnp.int32, (TQ, TK), 1)
    kb = TQ // TK
    heads = [slice(hh * HEAD_DIM, (hh + 1) * HEAD_DIM) for hh in range(2)]
    qs = [q_ref[:, sl] * scale for sl in heads]

    def block(q, s0, sl, o, r, mask):
        k = k_ref[pl.ds(s0, TK), sl]
        v = v_ref[pl.ds(s0, TK), sl]
        z = lax.dot_general(q, k, (((1,), (1,)), ((), ())), preferred_element_type=F32)
        l1m = -_softplus(z)
        if mask is not None:
            l1m = jnp.where(mask, l1m, 0.0)
        l_hi = l1m.astype(BF16)
        l_lo = (l1m - l_hi.astype(F32)).astype(BF16)
        c = (jnp.dot(l_hi, tri, preferred_element_type=F32)
             + jnp.dot(l_lo, tri, preferred_element_type=F32) + r)
        a = jnp.exp(z + c)
        if mask is not None:
            a = jnp.where(mask, a, 0.0)
        o = o + jnp.dot(a.astype(BF16), v, preferred_element_type=F32)
        return o, c[:, 0:1]

    carry = tuple((jnp.zeros((TQ, HEAD_DIM), F32), jnp.zeros((TQ, 1), F32)) for _ in range(2))
    for d in reversed(range(kb)):
        s0 = pl.multiple_of(i * TQ + d * TK, TK)
        mask = rows > cols + d * TK
        carry = tuple(block(qs[hh], s0, heads[hh], carry[hh][0], carry[hh][1], mask)
                      for hh in range(2))

    def body(jj, carry):
        s0 = pl.multiple_of((i * kb - 1 - jj) * TK, TK)
        return tuple(block(qs[hh], s0, heads[hh], carry[hh][0], carry[hh][1], None)
                     for hh in range(2))

    carry = lax.fori_loop(0, i * kb, body, carry)
    o_ref[...] = jnp.concatenate([carry[0][0], carry[1][0]], axis=1)


def _attention(qkv3):
    b, s, _ = qkv3.shape
    hp = SB_HEADS // 2
    return pl.pallas_call(
        _attn_kernel,
        grid=(b, hp, s // TQ),
        in_specs=[
            pl.BlockSpec((None, TQ, LANES), lambda bi, h, i: (bi, i, h)),
            pl.BlockSpec((None, s, LANES), lambda bi, h, i: (bi, 0, hp + h)),
            pl.BlockSpec((None, s, LANES), lambda bi, h, i: (bi, 0, 2 * hp + h)),
        ],
        out_specs=pl.BlockSpec((None, TQ, LANES), lambda bi, h, i: (bi, i, h)),
        out_shape=jax.ShapeDtypeStruct((b, s, SB_WIDTH), F32),
        compiler_params=pltpu.CompilerParams(
            dimension_semantics=("parallel", "parallel", "arbitrary"),
            vmem_limit_bytes=VMEM_LIMIT),
        name="sb_attention",
    )(qkv3, qkv3, qkv3)


def _topk_rows(s, k):
    n = s.shape[0]
    iota = lax.broadcasted_iota(jnp.int32, s.shape, 0).astype(F32)
    rank = jnp.full(s.shape, float(k), F32)
    vals, idxs = [], []
    for r in range(k):
        m = jnp.max(s, axis=0, keepdims=True)
        im = jnp.min(jnp.where(s == m, iota, float(n)), axis=0, keepdims=True)
        hit = iota == im
        vals.append(m)
        idxs.append(im)
        rank = jnp.where(hit, float(r), rank)
        s = jnp.where(hit, -jnp.inf, s)
    return jnp.concatenate(vals, axis=0), jnp.concatenate(idxs, axis=0), rank


def _mix_route_kernel(tiles_per_seq, x_ref, oa_ref, gb_ref, gc_ref, hc_ref, gcp_ref, hcp_ref,
                      cw_ref, cb_ref, ag_ref, cg_ref, wo_ref, fg_ref, wq_ref, keys_ref,
                      x1_ref, xnt_ref, cnt_ref, e1_ref, rk2_ref, e2_ref, q_sc):
    i = pl.program_id(0)
    tm = x_ref.shape[0]
    u = gc_ref[...] * hc_ref[...]
    first = (i % tiles_per_seq) == 0
    u_prev = jnp.where(first, 0.0, gcp_ref[...] * hcp_ref[...])
    u_ext = jnp.concatenate([u_prev, u], axis=0)
    cw = cw_ref[...]
    y = (cw[0:1, :] * u_ext[SUBLANES - 2:SUBLANES - 2 + tm, :]
         + cw[1:2, :] * u_ext[SUBLANES - 1:SUBLANES - 1 + tm, :]
         + cw[2:3, :] * u) + cb_ref[...]
    o_conv = gb_ref[...] * y
    mixed = jnp.concatenate([_rms(oa_ref[...], ag_ref[...]), _rms(o_conv, cg_ref[...])], axis=1)
    x1 = x_ref[...] + jnp.dot(mixed.astype(BF16), wo_ref[...], preferred_element_type=F32)
    x1_ref[...] = x1
    xn = _rms(x1, fg_ref[...]).astype(BF16)
    xnt_ref[...] = xn.T
    q_sc[...] = jnp.dot(xn, wq_ref[...], preferred_element_type=F32)

    def head(h, carry):
        c0 = pl.multiple_of(h * PEER_DK, PEER_DK)
        q1 = q_sc[:, pl.ds(c0, D_HALF)].astype(BF16)
        q2 = q_sc[:, pl.ds(c0 + D_HALF, D_HALF)].astype(BF16)
        nt = (((1,), (1,)), ((), ()))
        s1 = lax.dot_general(keys_ref[0, h], q1, nt, preferred_element_type=F32)
        s2 = lax.dot_general(keys_ref[1, h], q2, nt, preferred_element_type=F32)
        v1, _, rk1 = _topk_rows(s1, PEER_TOPK)
        v2, _, rk2 = _topk_rows(s2, PEER_TOPK)
        widths = [PEER_TOPK // (a + 1) for a in range(PEER_TOPK)]
        n_cand = sum(widths)
        pad = -n_cand % SUBLANES
        cand = jnp.concatenate(
            [v1[a:a + 1, :] + v2[0:widths[a], :] for a in range(PEER_TOPK)]
            + [jnp.full((pad, tm), -jnp.inf, F32)], axis=0)
        iota = lax.broadcasted_iota(jnp.int32, cand.shape, 0).astype(F32)
        left = cand
        z = jnp.zeros((1, tm), F32)
        for r in range(PEER_TOPK):
            m = jnp.max(left, axis=0, keepdims=True)
            if r == 0:
                m0 = m
            im = jnp.min(jnp.where(left == m, iota, float(n_cand + pad)), axis=0, keepdims=True)
            left = jnp.where(iota == im, -jnp.inf, left)
            z = z + jnp.exp(m - m0)
        picked = jnp.where(left == cand, 0.0, 1.0)
        cnt = jnp.zeros_like(s1)
        off = 0
        for a in range(PEER_TOPK):
            n_a = jnp.sum(picked[off:off + widths[a], :], axis=0, keepdims=True)
            cnt = cnt + jnp.where(rk1 == float(a), n_a, 0.0)
            off += widths[a]
        cnt_ref[h] = cnt
        e1_ref[h] = jnp.exp(s1 - v1[0:1, :])
        rk2_ref[h] = rk2.astype(BF16)
        e2_ref[h] = (jnp.exp(s2 - v2[0:1, :]) * (0.5 / z)).astype(BF16)
        return carry

    lax.fori_loop(0, PEER_HEADS, head, 0)


def _mix_route(x2, oa, cv, conv_w, conv_b, attn_g, conv_g, wo_bf16, ffn_g, wq_bf16, keys_bf16, seq):
    m = x2.shape[0]
    tm = TM_MIX
    row = lambda i: (i, 0)
    const2 = lambda i: (0, 0)
    halo = tm // SUBLANES
    return pl.pallas_call(
        functools.partial(_mix_route_kernel, seq // tm),
        grid=(m // tm,),
        in_specs=[
            pl.BlockSpec((tm, D_MODEL), row),
            pl.BlockSpec((tm, SB_WIDTH), row),
            pl.BlockSpec((tm, CONV_WIDTH), lambda i: (i, 0)),
            pl.BlockSpec((tm, CONV_WIDTH), lambda i: (i, 1)),
            pl.BlockSpec((tm, CONV_WIDTH), lambda i: (i, 2)),
            pl.BlockSpec((SUBLANES, CONV_WIDTH), lambda i: (jnp.maximum(i * halo - 1, 0), 1)),
            pl.BlockSpec((SUBLANES, CONV_WIDTH), lambda i: (jnp.maximum(i * halo - 1, 0), 2)),
            pl.BlockSpec((CONV_K, CONV_WIDTH), const2),
            pl.BlockSpec((1, CONV_WIDTH), const2),
            pl.BlockSpec((1, SB_WIDTH), const2),
            pl.BlockSpec((1, CONV_WIDTH), const2),
            pl.BlockSpec((D_MODEL, D_MODEL), const2),
            pl.BlockSpec((1, D_MODEL), const2),
            pl.BlockSpec((D_MODEL, PEER_HEADS * PEER_DK), const2),
            pl.BlockSpec((2, PEER_HEADS, N_KEYS, D_HALF), lambda i: (0, 0, 0, 0)),
        ],
        out_specs=[
            pl.BlockSpec((tm, D_MODEL), row),
            pl.BlockSpec((D_MODEL, tm), lambda i: (0, i)),
        ] + [pl.BlockSpec((PEER_HEADS, N_KEYS, tm), lambda i: (0, 0, i))] * 4,
        out_shape=[
            jax.ShapeDtypeStruct((m, D_MODEL), F32),
            jax.ShapeDtypeStruct((D_MODEL, m), BF16),
        ] + [jax.ShapeDtypeStruct((PEER_HEADS, N_KEYS, m), dt) for dt in (F32, F32, BF16, BF16)],
        scratch_shapes=[pltpu.VMEM((tm, PEER_HEADS * PEER_DK), F32)],
        compiler_params=pltpu.CompilerParams(
            dimension_semantics=("parallel",), vmem_limit_bytes=VMEM_LIMIT),
        name="mix_route",
    )(x2, oa, cv, cv, cv, cv, cv, conv_w, conv_b.reshape(1, -1), attn_g.reshape(1, -1),
      conv_g.reshape(1, -1), wo_bf16, ffn_g.reshape(1, -1), wq_bf16, keys_bf16)


def _peer_kernel(xnt_ref, u_ref, vt_ref, cnt_ref, e1_ref, rk2_ref, e2_ref, out_ref, acc_sc):
    e = pl.program_id(1)

    @pl.when(e == 0)
    def _():
        acc_sc[...] = jnp.zeros_like(acc_sc)

    xnt = xnt_ref[...]
    rows = TE_PEER // N_KEYS
    w_parts = []
    for r in range(rows):
        sl = slice(r * N_KEYS, (r + 1) * N_KEYS)
        act = jnp.dot(u_ref[sl, :], xnt, preferred_element_type=F32)
        gate = jnp.zeros(act.shape, BF16)
        for h in range(PEER_HEADS):
            picked = rk2_ref[h] < cnt_ref[h, r:r + 1, :].astype(BF16)
            gate = gate + jnp.where(picked, e2_ref[h] * e1_ref[h, r:r + 1, :].astype(BF16), 0.0)
        inner = act * (GELU_C0 + (GELU_C0 * GELU_C1) * (act * act))
        w_parts.append((act * (1.0 + jnp.tanh(inner))).astype(BF16) * gate)
    w = jnp.concatenate(w_parts, axis=0)
    acc_sc[...] += jnp.dot(vt_ref[...], w, preferred_element_type=F32)

    @pl.when(e == pl.num_programs(1) - 1)
    def _():
        out_ref[...] = acc_sc[...].T


def _peer_experts(xnt, u_bf16, vt_bf16, cnt, e1, rk2, e2):
    m = xnt.shape[1]
    n_exp = u_bf16.shape[0]
    tm, te = TM_PEER, TE_PEER
    rows = te // N_KEYS
    per_row = pl.BlockSpec((PEER_HEADS, rows, tm), lambda t, e: (0, e, t))
    per_key = pl.BlockSpec((PEER_HEADS, N_KEYS, tm), lambda t, e: (0, 0, t))
    return pl.pallas_call(
        _peer_kernel,
        grid=(m // tm, n_exp // te),
        in_specs=[
            pl.BlockSpec((D_MODEL, tm), lambda t, e: (0, t)),
            pl.BlockSpec((te, D_MODEL), lambda t, e: (e, 0)),
            pl.BlockSpec((D_MODEL, te), lambda t, e: (0, e)),
            per_row, per_row, per_key, per_key,
        ],
        out_specs=pl.BlockSpec((tm, D_MODEL), lambda t, e: (t, 0)),
        out_shape=jax.ShapeDtypeStruct((m, D_MODEL), F32),
        scratch_shapes=[pltpu.VMEM((D_MODEL, tm), F32)],
        compiler_params=pltpu.CompilerParams(
            dimension_semantics=("parallel", "arbitrary"), vmem_limit_bytes=VMEM_LIMIT),
        name="peer_experts",
    )(xnt, u_bf16, vt_bf16, cnt, e1, rk2, e2)


def _final_kernel(x1_ref, p_ref, g_ref, o_ref):
    o_ref[...] = _rms(x1_ref[...] + p_ref[...], g_ref[...])


def _residual_norm(x1, p, g):
    m = x1.shape[0]
    row = lambda i: (i, 0)
    return pl.pallas_call(
        _final_kernel,
        grid=(m // TM_FINAL,),
        in_specs=[pl.BlockSpec((TM_FINAL, D_MODEL), row), pl.BlockSpec((TM_FINAL, D_MODEL), row),
                  pl.BlockSpec((1, D_MODEL), lambda i: (0, 0))],
        out_specs=pl.BlockSpec((TM_FINAL, D_MODEL), row),
        out_shape=jax.ShapeDtypeStruct((m, D_MODEL), F32),
        compiler_params=pltpu.CompilerParams(dimension_semantics=("parallel",)),
        name="residual_norm",
    )(x1, p, g.reshape(1, D_MODEL))


def kernel(x, mix_norm_g, w_in, conv_w, conv_b, attn_out_g, conv_out_g, w_out, ffn_norm_g,
           peer_wq, peer_keys, peer_u, peer_v, final_norm_g):
    b, s, d = x.shape
    assert w_in.shape[0] == 1, "single-layer block"
    w_in_b = w_in[0].astype(BF16)
    w_out_b = w_out[0].astype(BF16)
    wq_b = peer_wq[0].astype(BF16)
    keys_b = peer_keys[0].astype(BF16)
    u_b = peer_u[0].astype(BF16)
    vt_b = peer_v[0].T.astype(BF16)
    m = b * s
    x2 = x.reshape(m, d)
    qkv, cv = _inproj(x2, mix_norm_g[0], w_in_b)
    oa = _attention(qkv.reshape(b, s, 3 * SB_WIDTH)).reshape(m, SB_WIDTH)
    x1, xnt, cnt, e1, rk2, e2 = _mix_route(
        x2, oa, cv, conv_w[0], conv_b[0], attn_out_g[0], conv_out_g[0],
        w_out_b, ffn_norm_g[0], wq_b, keys_b, s)
    p = _peer_experts(xnt, u_b, vt_b, cnt, e1, rk2, e2)
    return _residual_norm(x1, p, final_norm_g).reshape(b, s, d)
```

```python
import functools
import math

import jax
import jax.numpy as jnp
from jax import lax
from jax.experimental import pallas as pl
from jax.experimental.pallas import tpu as pltpu

F32 = jnp.float32
BF16 = jnp.bfloat16

D_MODEL = 1024
SB_HEADS = 8
HEAD_DIM = 64
SB_WIDTH = SB_HEADS * HEAD_DIM
CONV_WIDTH = D_MODEL - SB_WIDTH
CONV_K = 3
N_KEYS = 128
PEER_HEADS = 8
PEER_DK = 256
D_HALF = PEER_DK // 2
PEER_TOPK = 16
N_PICKS = PEER_HEADS * PEER_TOPK
EPS = 1e-6
GELU_C0 = math.sqrt(2.0 / math.pi)
GELU_C1 = 0.044715

SUBLANES = 8
LANES = 128
ROW_TILES = D_MODEL // LANES

TM_PROJ = 512
TQ = 512
TK = 256
TM_MIX = 256
TM_PEER = 512
TE_PEER = 2048
VMEM_LIMIT = 56 * 1024 * 1024


def _rms(x, g):
    return x * lax.rsqrt(jnp.mean(x * x, axis=-1, keepdims=True) + EPS) * g


def _inproj_kernel(x_ref, g_ref, w_ref, qkv_ref, cv_ref):
    h = _rms(x_ref[...], g_ref[...])
    p = jnp.dot(h.astype(BF16), w_ref[...], preferred_element_type=F32)
    qkv_ref[...] = p[:, : 3 * SB_WIDTH].astype(BF16)
    cv_ref[...] = p[:, 3 * SB_WIDTH:]


def _inproj(x2, g, w_bf16):
    m = x2.shape[0]
    n = w_bf16.shape[1]
    return pl.pallas_call(
        _inproj_kernel,
        grid=(m // TM_PROJ,),
        in_specs=[
            pl.BlockSpec((TM_PROJ, D_MODEL), lambda i: (i, 0)),
            pl.BlockSpec((1, D_MODEL), lambda i: (0, 0)),
            pl.BlockSpec((D_MODEL, n), lambda i: (0, 0)),
        ],
        out_specs=[
            pl.BlockSpec((TM_PROJ, 3 * SB_WIDTH), lambda i: (i, 0)),
            pl.BlockSpec((TM_PROJ, 3 * CONV_WIDTH), lambda i: (i, 0)),
        ],
        out_shape=[
            jax.ShapeDtypeStruct((m, 3 * SB_WIDTH), BF16),
            jax.ShapeDtypeStruct((m, 3 * CONV_WIDTH), F32),
        ],
        compiler_params=pltpu.CompilerParams(
            dimension_semantics=("parallel",), vmem_limit_bytes=VMEM_LIMIT),
        name="inproj",
    )(x2, g.reshape(1, D_MODEL), w_bf16)


def _softplus(z):
    return jnp.maximum(z, 0.0) + jnp.log(1.0 + jnp.exp(-jnp.abs(z)))


def _attn_kernel(q_ref, k_ref, v_ref, o_ref):
    i = pl.program_id(2)
    scale = 1.0 / math.sqrt(HEAD_DIM)
    kr = lax.broadcasted_iota(jnp.int32, (TK, TK), 0)
    kc = lax.broadcasted_iota(jnp.int32, (TK, TK), 1)
    tri = (kr >= kc).astype(BF16)
    rows = lax.broadcasted_iota(jnp.int32, (TQ, TK), 0)
    cols = lax.broadcasted_iota(jnp.int32, (TQ, TK), 1)
    kb = TQ // TK
    heads = [slice(hh * HEAD_DIM, (hh + 1) * HEAD_DIM) for hh in range(2)]
    qs = [q_ref[:, sl] * scale for sl in heads]

    def block(q, s0, sl, o, r, mask):
        k = k_ref[pl.ds(s0, TK), sl]
        v = v_ref[pl.ds(s0, TK), sl]
        z = lax.dot_general(q, k, (((1,), (1,)), ((), ())), preferred_element_type=F32)
        l1m = -_softplus(z)
        if mask is not None:
            l1m = jnp.where(mask, l1m, 0.0)
        l_hi = l1m.astype(BF16)
        l_lo = (l1m - l_hi.astype(F32)).astype(BF16)
        c = (jnp.dot(l_hi, tri, preferred_element_type=F32)
             + jnp.dot(l_lo, tri, preferred_element_type=F32) + r)
        a = jnp.exp(z + c)
        if mask is not None:
            a = jnp.where(mask, a, 0.0)
        o = o + jnp.dot(a.astype(BF16), v, preferred_element_type=F32)
        return o, c[:, 0:1]

    carry = tuple((jnp.zeros((TQ, HEAD_DIM), F32), jnp.zeros((TQ, 1), F32)) for _ in range(2))
    for d in reversed(range(kb)):
        s0 = pl.multiple_of(i * TQ + d * TK, TK)
        mask = rows > cols + d * TK
        carry = tuple(block(qs[hh], s0, heads[hh], carry[hh][0], carry[hh][1], mask)
                      for hh in range(2))

    def body(jj, carry):
        s0 = pl.multiple_of((i * kb - 1 - jj) * TK, TK)
        return tuple(block(qs[hh], s0, heads[hh], carry[hh][0], carry[hh][1], None)
                     for hh in range(2))

    carry = lax.fori_loop(0, i * kb, body, carry)
    o_ref[...] = jnp.concatenate([carry[0][0], carry[1][0]], axis=1)


def _attention(qkv3):
    b, s, _ = qkv3.shape
    hp = SB_HEADS // 2
    return pl.pallas_call(
        _attn_kernel,
        grid=(b, hp, s // TQ),
        in_specs=[
            pl.BlockSpec((None, TQ, LANES), lambda bi, h, i: (bi, i, h)),
            pl.BlockSpec((None, s, LANES), lambda bi, h, i: (bi, 0, hp + h)),
            pl.BlockSpec((None, s, LANES), lambda bi, h, i: (bi, 0, 2 * hp + h)),
        ],
        out_specs=pl.BlockSpec((None, TQ, LANES), lambda bi, h, i: (bi, i, h)),
        out_shape=jax.ShapeDtypeStruct((b, s, SB_WIDTH), F32),
        compiler_params=pltpu.CompilerParams(
            dimension_semantics=("parallel", "parallel", "arbitrary"),
            vmem_limit_bytes=VMEM_LIMIT),
        name="sb_attention",
    )(qkv3, qkv3, qkv3)


def _topk_rows(s, k):
    n = s.shape[0]
    iota = lax.broadcasted_iota(jnp.int32, s.shape, 0).astype(F32)
    rank = jnp.full(s.shape, float(k), F32)
    vals, idxs = [], []
    for r in range(k):
        m = jnp.max(s, axis=0, keepdims=True)
        im = jnp.min(jnp.where(s == m, iota, float(n)), axis=0, keepdims=True)
        hit = iota == im
        vals.append(m)
        idxs.append(im)
        rank = jnp.where(hit, float(r), rank)
        s = jnp.where(hit, -jnp.inf, s)
    return jnp.concatenate(vals, axis=0), jnp.concatenate(idxs, axis=0), rank


def _mix_route_kernel(tiles_per_seq, x_ref, oa_ref, gb_ref, gc_ref, hc_ref, gcp_ref, hcp_ref,
                      cw_ref, cb_ref, ag_ref, cg_ref, wo_ref, fg_ref, wq_ref, keys_ref,
                      x1_ref, xnt_ref, cnt_ref, e1_ref, rk2_ref, e2_ref, q_sc):
    i = pl.program_id(0)
    tm = x_ref.shape[0]
    u = gc_ref[...] * hc_ref[...]
    first = (i % tiles_per_seq) == 0
    u_prev = jnp.where(first, 0.0, gcp_ref[...] * hcp_ref[...])
    u_ext = jnp.concatenate([u_prev, u], axis=0)
    cw = cw_ref[...]
    y = (cw[0:1, :] * u_ext[SUBLANES - 2:SUBLANES - 2 + tm, :]
         + cw[1:2, :] * u_ext[SUBLANES - 1:SUBLANES - 1 + tm, :]
         + cw[2:3, :] * u) + cb_ref[...]
    o_conv = gb_ref[...] * y
    mixed = jnp.concatenate([_rms(oa_ref[...], ag_ref[...]), _rms(o_conv, cg_ref[...])], axis=1)
    x1 = x_ref[...] + jnp.dot(mixed.astype(BF16), wo_ref[...], preferred_element_type=F32)
    x1_ref[...] = x1
    xn = _rms(x1, fg_ref[...]).astype(BF16)
    xnt_ref[...] = xn.T
    q_sc[...] = jnp.dot(xn, wq_ref[...], preferred_element_type=F32)

    def head(h, carry):
        c0 = pl.multiple_of(h * PEER_DK, PEER_DK)
        q1 = q_sc[:, pl.ds(c0, D_HALF)].astype(BF16)
        q2 = q_sc[:, pl.ds(c0 + D_HALF, D_HALF)].astype(BF16)
        nt = (((1,), (1,)), ((), ()))
        s1 = lax.dot_general(keys_ref[0, h], q1, nt, preferred_element_type=F32)
        s2 = lax.dot_general(keys_ref[1, h], q2, nt, preferred_element_type=F32)
        v1, _, rk1 = _topk_rows(s1, PEER_TOPK)
        v2, _, rk2 = _topk_rows(s2, PEER_TOPK)
        widths = [PEER_TOPK // (a + 1) for a in range(PEER_TOPK)]
        n_cand = sum(widths)
        pad = -n_cand % SUBLANES
        cand = jnp.concatenate(
            [v1[a:a + 1, :] + v2[0:widths[a], :] for a in range(PEER_TOPK)]
            + [jnp.full((pad, tm), -jnp.inf, F32)], axis=0)
        iota = lax.broadcasted_iota(jnp.int32, cand.shape, 0).astype(F32)
        left = cand
        z = jnp.zeros((1, tm), F32)
        for r in range(PEER_TOPK):
            m = jnp.max(left, axis=0, keepdims=True)
            if r == 0:
                m0 = m
            im = jnp.min(jnp.where(left == m, iota, float(n_cand + pad)), axis=0, keepdims=True)
            left = jnp.where(iota == im, -jnp.inf, left)
            z = z + jnp.exp(m - m0)
        picked = jnp.where(left == cand, 0.0, 1.0)
        cnt = jnp.zeros_like(s1)
        off = 0
        for a in range(PEER_TOPK):
            n_a = jnp.sum(picked[off:off + widths[a], :], axis=0, keepdims=True)
            cnt = cnt + jnp.where(rk1 == float(a), n_a, 0.0)
            off += widths[a]
        cnt_ref[h] = cnt
        e1_ref[h] = jnp.exp(s1 - v1[0:1, :])
        rk2_ref[h] = rk2.astype(BF16)
        e2_ref[h] = (jnp.exp(s2 - v2[0:1, :]) * (0.5 / z)).astype(BF16)
        return carry

    lax.fori_loop(0, PEER_HEADS, head, 0)


def _mix_route(x2, oa, cv, conv_w, conv_b, attn_g, conv_g, wo_bf16, ffn_g, wq_bf16, keys_bf16, seq):
    m = x2.shape[0]
    tm = TM_MIX
    row = lambda i: (i, 0)
    const2 = lambda i: (0, 0)
    halo = tm // SUBLANES
    return pl.pallas_call(
        functools.partial(_mix_route_kernel, seq // tm),
        grid=(m // tm,),
        in_specs=[
            pl.BlockSpec((tm, D_MODEL), row),
            pl.BlockSpec((tm, SB_WIDTH), row),
            pl.BlockSpec((tm, CONV_WIDTH), lambda i: (i, 0)),
            pl.BlockSpec((tm, CONV_WIDTH), lambda i: (i, 1)),
            pl.BlockSpec((tm, CONV_WIDTH), lambda i: (i, 2)),
            pl.BlockSpec((SUBLANES, CONV_WIDTH), lambda i: (jnp.maximum(i * halo - 1, 0), 1)),
            pl.BlockSpec((SUBLANES, CONV_WIDTH), lambda i: (jnp.maximum(i * halo - 1, 0), 2)),
            pl.BlockSpec((CONV_K, CONV_WIDTH), const2),
            pl.BlockSpec((1, CONV_WIDTH), const2),
            pl.BlockSpec((1, SB_WIDTH), const2),
            pl.BlockSpec((1, CONV_WIDTH), const2),
            pl.BlockSpec((D_MODEL, D_MODEL), const2),
            pl.BlockSpec((1, D_MODEL), const2),
            pl.BlockSpec((D_MODEL, PEER_HEADS * PEER_DK), const2),
            pl.BlockSpec((2, PEER_HEADS, N_KEYS, D_HALF), lambda i: (0, 0, 0, 0)),
        ],
        out_specs=[
            pl.BlockSpec((tm, D_MODEL), row),
            pl.BlockSpec((D_MODEL, tm), lambda i: (0, i)),
        ] + [pl.BlockSpec((PEER_HEADS, N_KEYS, tm), lambda i: (0, 0, i))] * 4,
        out_shape=[
            jax.ShapeDtypeStruct((m, D_MODEL), F32),
            jax.ShapeDtypeStruct((D_MODEL, m), BF16),
        ] + [jax.ShapeDtypeStruct((PEER_HEADS, N_KEYS, m), dt) for dt in (F32, F32, BF16, BF16)],
        scratch_shapes=[pltpu.VMEM((tm, PEER_HEADS * PEER_DK), F32)],
        compiler_params=pltpu.CompilerParams(
            dimension_semantics=("parallel",), vmem_limit_bytes=VMEM_LIMIT),
        name="mix_route",
    )(x2, oa, cv, cv, cv, cv, cv, conv_w, conv_b.reshape(1, -1), attn_g.reshape(1, -1),
      conv_g.reshape(1, -1), wo_bf16, ffn_g.reshape(1, -1), wq_bf16, keys_bf16)


def _peer_kernel(xnt_ref, u_ref, vt_ref, cnt_ref, e1_ref, rk2_ref, e2_ref, x1_ref, fg_ref,
                 out_ref, acc_sc):
    e = pl.program_id(1)

    @pl.when(e == 0)
    def _():
        acc_sc[...] = jnp.zeros_like(acc_sc)

    xnt = xnt_ref[...]
    rows = TE_PEER // N_KEYS
    w_parts = []
    for r in range(rows):
        sl = slice(r * N_KEYS, (r + 1) * N_KEYS)
        act = jnp.dot(u_ref[sl, :], xnt, preferred_element_type=F32)
        gate = jnp.zeros(act.shape, BF16)
        for h in range(PEER_HEADS):
            picked = rk2_ref[h] < cnt_ref[h, r:r + 1, :].astype(BF16)
            gate = gate + jnp.where(picked, e2_ref[h] * e1_ref[h, r:r + 1, :].astype(BF16), 0.0)
        inner = act * (GELU_C0 + (GELU_C0 * GELU_C1) * (act * act))
        w_parts.append((act * (1.0 + jnp.tanh(inner))).astype(BF16) * gate)
    w = jnp.concatenate(w_parts, axis=0)
    acc_sc[...] += jnp.dot(vt_ref[...], w, preferred_element_type=F32)

    @pl.when(e == pl.num_programs(1) - 1)
    def _():
        out_ref[...] = _rms(x1_ref[...] + acc_sc[...].T, fg_ref[...])


def _peer_experts(xnt, u_bf16, vt_bf16, cnt, e1, rk2, e2, x1, final_g):
    m = xnt.shape[1]
    n_exp = u_bf16.shape[0]
    tm, te = TM_PEER, TE_PEER
    rows = te // N_KEYS
    per_row = pl.BlockSpec((PEER_HEADS, rows, tm), lambda t, e: (0, e, t))
    per_key = pl.BlockSpec((PEER_HEADS, N_KEYS, tm), lambda t, e: (0, 0, t))
    return pl.pallas_call(
        _peer_kernel,
        grid=(m // tm, n_exp // te),
        in_specs=[
            pl.BlockSpec((D_MODEL, tm), lambda t, e: (0, t)),
            pl.BlockSpec((te, D_MODEL), lambda t, e: (e, 0)),
            pl.BlockSpec((D_MODEL, te), lambda t, e: (0, e)),
            per_row, per_row, per_key, per_key,
            pl.BlockSpec((tm, D_MODEL), lambda t, e: (t, 0)),
            pl.BlockSpec((1, D_MODEL), lambda t, e: (0, 0)),
        ],
        out_specs=pl.BlockSpec((tm, D_MODEL), lambda t, e: (t, 0)),
        out_shape=jax.ShapeDtypeStruct((m, D_MODEL), F32),
        scratch_shapes=[pltpu.VMEM((D_MODEL, tm), F32)],
        compiler_params=pltpu.CompilerParams(
            dimension_semantics=("parallel", "arbitrary"), vmem_limit_bytes=VMEM_LIMIT),
        name="peer_experts",
    )(xnt, u_bf16, vt_bf16, cnt, e1, rk2, e2, x1, final_g.reshape(1, D_MODEL))


def kernel(x, mix_norm_g, w_in, conv_w, conv_b, attn_out_g, conv_out_g, w_out, ffn_norm_g,
           peer_wq, peer_keys, peer_u, peer_v, final_norm_g):
    b, s, d = x.shape
    assert w_in.shape[0] == 1, "single-layer block"
    w_in_b = w_in[0].astype(BF16)
    w_out_b = w_out[0].astype(BF16)
    wq_b = peer_wq[0].astype(BF16)
    keys_b = peer_keys[0].astype(BF16)
    u_b = peer_u[0].astype(BF16)
    vt_b = peer_v[0].T.astype(BF16)
    m = b * s
    x2 = x.reshape(m, d)
    qkv, cv = _inproj(x2, mix_norm_g[0], w_in_b)
    oa = _attention(qkv.reshape(b, s, 3 * SB_WIDTH)).reshape(m, SB_WIDTH)
    x1, xnt, cnt, e1, rk2, e2 = _mix_route(
        x2, oa, cv, conv_w[0], conv_b[0], attn_out_g[0], conv_out_g[0],
        w_out_b, ffn_norm_g[0], wq_b, keys_b, s)
    out = _peer_experts(xnt, u_b, vt_b, cnt, e1, rk2, e2, x1, final_norm_g)
    return out.reshape(b, s, d)
```

```python
import functools
import math

import jax
import jax.numpy as jnp
from jax import lax
from jax.experimental import pallas as pl
from jax.experimental.pallas import tpu as pltpu

F32 = jnp.float32
BF16 = jnp.bfloat16

D_MODEL = 1024
SB_HEADS = 8
HEAD_DIM = 64
SB_WIDTH = SB_HEADS * HEAD_DIM
CONV_WIDTH = D_MODEL - SB_WIDTH
CONV_K = 3
N_KEYS = 128
PEER_HEADS = 8
PEER_DK = 256
D_HALF = PEER_DK // 2
PEER_TOPK = 16
N_PICKS = PEER_HEADS * PEER_TOPK
EPS = 1e-6
GELU_C0 = math.sqrt(2.0 / math.pi)
GELU_C1 = 0.044715

SUBLANES = 8
LANES = 128
ROW_TILES = D_MODEL // LANES

TM_PROJ = 512
TQ = 512
TK = 256
TM_MIX = 512
TM_PEER = 512
TE_PEER = 2048
VMEM_LIMIT = 56 * 1024 * 1024


def _rms(x, g):
    return x * lax.rsqrt(jnp.mean(x * x, axis=-1, keepdims=True) + EPS) * g


def _inproj_kernel(x_ref, g_ref, w_ref, qkv_ref, cv_ref):
    h = _rms(x_ref[...], g_ref[...])
    p = jnp.dot(h.astype(BF16), w_ref[...], preferred_element_type=F32)
    qkv_ref[...] = p[:, : 3 * SB_WIDTH].astype(BF16)
    cv_ref[...] = p[:, 3 * SB_WIDTH:]


def _inproj(x2, g, w_bf16):
    m = x2.shape[0]
    n = w_bf16.shape[1]
    return pl.pallas_call(
        _inproj_kernel,
        grid=(m // TM_PROJ,),
        in_specs=[
            pl.BlockSpec((TM_PROJ, D_MODEL), lambda i: (i, 0)),
            pl.BlockSpec((1, D_MODEL), lambda i: (0, 0)),
            pl.BlockSpec((D_MODEL, n), lambda i: (0, 0)),
        ],
        out_specs=[
            pl.BlockSpec((TM_PROJ, 3 * SB_WIDTH), lambda i: (i, 0)),
            pl.BlockSpec((TM_PROJ, 3 * CONV_WIDTH), lambda i: (i, 0)),
        ],
        out_shape=[
            jax.ShapeDtypeStruct((m, 3 * SB_WIDTH), BF16),
            jax.ShapeDtypeStruct((m, 3 * CONV_WIDTH), F32),
        ],
        compiler_params=pltpu.CompilerParams(
            dimension_semantics=("parallel",), vmem_limit_bytes=VMEM_LIMIT),
        name="inproj",
    )(x2, g.reshape(1, D_MODEL), w_bf16)


def _softplus(z):
    return jnp.maximum(z, 0.0) + jnp.log(1.0 + jnp.exp(-jnp.abs(z)))


def _attn_kernel(q_ref, k_ref, v_ref, o_ref):
    i = pl.program_id(2)
    scale = 1.0 / math.sqrt(HEAD_DIM)
    kr = lax.broadcasted_iota(jnp.int32, (TK, TK), 0)
    kc = lax.broadcasted_iota(jnp.int32, (TK, TK), 1)
    tri = (kr >= kc).astype(BF16)
    rows = lax.broadcasted_iota(jnp.int32, (TQ, TK), 0)
    cols = lax.broadcasted_iota(jnp.int32, (TQ, TK), 1)
    kb = TQ // TK
    heads = [slice(hh * HEAD_DIM, (hh + 1) * HEAD_DIM) for hh in range(2)]
    qs = [q_ref[:, sl] * scale for sl in heads]

    def block(q, s0, sl, o, r, mask):
        k = k_ref[pl.ds(s0, TK), sl]
        v = v_ref[pl.ds(s0, TK), sl]
        z = lax.dot_general(q, k, (((1,), (1,)), ((), ())), preferred_element_type=F32)
        l1m = -_softplus(z)
        if mask is not None:
            l1m = jnp.where(mask, l1m, 0.0)
        l_hi = l1m.astype(BF16)
        l_lo = (l1m - l_hi.astype(F32)).astype(BF16)
        c = (jnp.dot(l_hi, tri, preferred_element_type=F32)
             + jnp.dot(l_lo, tri, preferred_element_type=F32) + r)
        a = jnp.exp(z + c)
        if mask is not None:
            a = jnp.where(mask, a, 0.0)
        o = o + jnp.dot(a.astype(BF16), v, preferred_element_type=F32)
        return o, c[:, 0:1]

    carry = tuple((jnp.zeros((TQ, HEAD_DIM), F32), jnp.zeros((TQ, 1), F32)) for _ in range(2))
    for d in reversed(range(kb)):
        s0 = pl.multiple_of(i * TQ + d * TK, TK)
        mask = rows > cols + d * TK
        carry = tuple(block(qs[hh], s0, heads[hh], carry[hh][0], carry[hh][1], mask)
                      for hh in range(2))

    def body(jj, carry):
        s0 = pl.multiple_of((i * kb - 1 - jj) * TK, TK)
        return tuple(block(qs[hh], s0, heads[hh], carry[hh][0], carry[hh][1], None)
                     for hh in range(2))

    carry = lax.fori_loop(0, i * kb, body, carry)
    o_ref[...] = jnp.concatenate([carry[0][0], carry[1][0]], axis=1)


def _attention(qkv3):
    b, s, _ = qkv3.shape
    hp = SB_HEADS // 2
    return pl.pallas_call(
        _attn_kernel,
        grid=(b, hp, s // TQ),
        in_specs=[
            pl.BlockSpec((None, TQ, LANES), lambda bi, h, i: (bi, i, h)),
            pl.BlockSpec((None, s, LANES), lambda bi, h, i: (bi, 0, hp + h)),
            pl.BlockSpec((None, s, LANES), lambda bi, h, i: (bi, 0, 2 * hp + h)),
        ],
        out_specs=pl.BlockSpec((None, TQ, LANES), lambda bi, h, i: (bi, i, h)),
        out_shape=jax.ShapeDtypeStruct((b, s, SB_WIDTH), F32),
        compiler_params=pltpu.CompilerParams(
            dimension_semantics=("parallel", "parallel", "arbitrary"),
            vmem_limit_bytes=VMEM_LIMIT),
        name="sb_attention",
    )(qkv3, qkv3, qkv3)


def _topk_rows(s, k):
    n = s.shape[0]
    iota = lax.broadcasted_iota(jnp.int32, s.shape, 0).astype(F32)
    rank = jnp.full(s.shape, float(k), F32)
    vals, idxs = [], []
    for r in range(k):
        m = jnp.max(s, axis=0, keepdims=True)
        im = jnp.min(jnp.where(s == m, iota, float(n)), axis=0, keepdims=True)
        hit = iota == im
        vals.append(m)
        idxs.append(im)
        rank = jnp.where(hit, float(r), rank)
        s = jnp.where(hit, -jnp.inf, s)
    return jnp.concatenate(vals, axis=0), jnp.concatenate(idxs, axis=0), rank


def _mix_route_kernel(tiles_per_seq, x_ref, oa_ref, gb_ref, gc_ref, hc_ref, gcp_ref, hcp_ref,
                      cw_ref, cb_ref, ag_ref, cg_ref, wo_ref, fg_ref, wq_ref, keys_ref,
                      x1_ref, xnt_ref, cnt_ref, e1_ref, rk2_ref, e2_ref, q_sc):
    i = pl.program_id(0)
    tm = x_ref.shape[0]
    u = gc_ref[...] * hc_ref[...]
    first = (i % tiles_per_seq) == 0
    u_prev = jnp.where(first, 0.0, gcp_ref[...] * hcp_ref[...])
    u_ext = jnp.concatenate([u_prev, u], axis=0)
    cw = cw_ref[...]
    y = (cw[0:1, :] * u_ext[SUBLANES - 2:SUBLANES - 2 + tm, :]
         + cw[1:2, :] * u_ext[SUBLANES - 1:SUBLANES - 1 + tm, :]
         + cw[2:3, :] * u) + cb_ref[...]
    o_conv = gb_ref[...] * y
    mixed = jnp.concatenate([_rms(oa_ref[...], ag_ref[...]), _rms(o_conv, cg_ref[...])], axis=1)
    x1 = x_ref[...] + jnp.dot(mixed.astype(BF16), wo_ref[...], preferred_element_type=F32)
    x1_ref[...] = x1
    xn = _rms(x1, fg_ref[...]).astype(BF16)
    xnt_ref[...] = xn.T
    q_sc[...] = jnp.dot(xn, wq_ref[...], preferred_element_type=F32)

    def head(h, carry):
        c0 = pl.multiple_of(h * PEER_DK, PEER_DK)
        q1 = q_sc[:, pl.ds(c0, D_HALF)].astype(BF16)
        q2 = q_sc[:, pl.ds(c0 + D_HALF, D_HALF)].astype(BF16)
        nt = (((1,), (1,)), ((), ()))
        s1 = lax.dot_general(keys_ref[0, h], q1, nt, preferred_element_type=F32)
        s2 = lax.dot_general(keys_ref[1, h], q2, nt, preferred_element_type=F32)
        v1, _, rk1 = _topk_rows(s1, PEER_TOPK)
        v2, _, rk2 = _topk_rows(s2, PEER_TOPK)
        widths = [PEER_TOPK // (a + 1) for a in range(PEER_TOPK)]
        n_cand = sum(widths)
        pad = -n_cand % SUBLANES
        cand = jnp.concatenate(
            [v1[a:a + 1, :] + v2[0:widths[a], :] for a in range(PEER_TOPK)]
            + [jnp.full((pad, tm), -jnp.inf, F32)], axis=0)
        iota = lax.broadcasted_iota(jnp.int32, cand.shape, 0).astype(F32)
        left = cand
        z = jnp.zeros((1, tm), F32)
        for r in range(PEER_TOPK):
            m = jnp.max(left, axis=0, keepdims=True)
            if r == 0:
                m0 = m
            im = jnp.min(jnp.where(left == m, iota, float(n_cand + pad)), axis=0, keepdims=True)
            left = jnp.where(iota == im, -jnp.inf, left)
            z = z + jnp.exp(m - m0)
        picked = jnp.where(left == cand, 0.0, 1.0)
        cnt = jnp.zeros_like(s1)
        off = 0
        for a in range(PEER_TOPK):
            n_a = jnp.sum(picked[off:off + widths[a], :], axis=0, keepdims=True)
            cnt = cnt + jnp.where(rk1 == float(a), n_a, 0.0)
            off += widths[a]
        cnt_ref[h] = cnt
        e1_ref[h] = jnp.exp(s1 - v1[0:1, :])
        rk2_ref[h] = rk2.astype(BF16)
        e2_ref[h] = (jnp.exp(s2 - v2[0:1, :]) * (0.5 / z)).astype(BF16)
        return carry

    lax.fori_loop(0, PEER_HEADS, head, 0)


def _mix_route(x2, oa, cv, conv_w, conv_b, attn_g, conv_g, wo_bf16, ffn_g, wq_bf16, keys_bf16, seq):
    m = x2.shape[0]
    tm = TM_MIX
    row = lambda i: (i, 0)
    const2 = lambda i: (0, 0)
    halo = tm // SUBLANES
    return pl.pallas_call(
        functools.partial(_mix_route_kernel, seq // tm),
        grid=(m // tm,),
        in_specs=[
            pl.BlockSpec((tm, D_MODEL), row),
            pl.BlockSpec((tm, SB_WIDTH), row),
            pl.BlockSpec((tm, CONV_WIDTH), lambda i: (i, 0)),
            pl.BlockSpec((tm, CONV_WIDTH), lambda i: (i, 1)),
            pl.BlockSpec((tm, CONV_WIDTH), lambda i: (i, 2)),
            pl.BlockSpec((SUBLANES, CONV_WIDTH), lambda i: (jnp.maximum(i * halo - 1, 0), 1)),
            pl.BlockSpec((SUBLANES, CONV_WIDTH), lambda i: (jnp.maximum(i * halo - 1, 0), 2)),
            pl.BlockSpec((CONV_K, CONV_WIDTH), const2),
            pl.BlockSpec((1, CONV_WIDTH), const2),
            pl.BlockSpec((1, SB_WIDTH), const2),
            pl.BlockSpec((1, CONV_WIDTH), const2),
            pl.BlockSpec((D_MODEL, D_MODEL), const2),
            pl.BlockSpec((1, D_MODEL), const2),
            pl.BlockSpec((D_MODEL, PEER_HEADS * PEER_DK), const2),
            pl.BlockSpec((2, PEER_HEADS, N_KEYS, D_HALF), lambda i: (0, 0, 0, 0)),
        ],
        out_specs=[
            pl.BlockSpec((tm, D_MODEL), row),
            pl.BlockSpec((D_MODEL, tm), lambda i: (0, i)),
        ] + [pl.BlockSpec((PEER_HEADS, N_KEYS, tm), lambda i: (0, 0, i))] * 4,
        out_shape=[
            jax.ShapeDtypeStruct((m, D_MODEL), F32),
            jax.ShapeDtypeStruct((D_MODEL, m), BF16),
        ] + [jax.ShapeDtypeStruct((PEER_HEADS, N_KEYS, m), dt) for dt in (F32, F32, BF16, BF16)],
        scratch_shapes=[pltpu.VMEM((tm, PEER_HEADS * PEER_DK), F32)],
        compiler_params=pltpu.CompilerParams(
            dimension_semantics=("parallel",), vmem_limit_bytes=VMEM_LIMIT),
        name="mix_route",
    )(x2, oa, cv, cv, cv, cv, cv, conv_w, conv_b.reshape(1, -1), attn_g.reshape(1, -1),
      conv_g.reshape(1, -1), wo_bf16, ffn_g.reshape(1, -1), wq_bf16, keys_bf16)


def _peer_kernel(xnt_ref, u_ref, vt_ref, cnt_ref, e1_ref, rk2_ref, e2_ref, x1_ref, fg_ref,
                 out_ref, acc_sc):
    e = pl.program_id(1)

    @pl.when(e == 0)
    def _():
        acc_sc[...] = jnp.zeros_like(acc_sc)

    xnt = xnt_ref[...]
    rows = TE_PEER // N_KEYS
    w_parts = []
    for r in range(rows):
        sl = slice(r * N_KEYS, (r + 1) * N_KEYS)
        act = jnp.dot(u_ref[sl, :], xnt, preferred_element_type=F32)
        gate = jnp.zeros(act.shape, BF16)
        for h in range(PEER_HEADS):
            picked = rk2_ref[h] < cnt_ref[h, r:r + 1, :].astype(BF16)
            gate = gate + jnp.where(picked, e2_ref[h] * e1_ref[h, r:r + 1, :].astype(BF16), 0.0)
        inner = act * (GELU_C0 + (GELU_C0 * GELU_C1) * (act * act))
        w_parts.append((act * (1.0 + jnp.tanh(inner))).astype(BF16) * gate)
    w = jnp.concatenate(w_parts, axis=0)
    acc_sc[...] += jnp.dot(vt_ref[...], w, preferred_element_type=F32)

    @pl.when(e == pl.num_programs(1) - 1)
    def _():
        out_ref[...] = _rms(x1_ref[...] + acc_sc[...].T, fg_ref[...])


def _peer_experts(xnt, u_bf16, vt_bf16, cnt, e1, rk2, e2, x1, final_g):
    m = xnt.shape[1]
    n_exp = u_bf16.shape[0]
    tm, te = TM_PEER, TE_PEER
    rows = te // N_KEYS
    per_row = pl.BlockSpec((PEER_HEADS, rows, tm), lambda t, e: (0, e, t))
    per_key = pl.BlockSpec((PEER_HEADS, N_KEYS, tm), lambda t, e: (0, 0, t))
    return pl.pallas_call(
        _peer_kernel,
        grid=(m // tm, n_exp // te),
        in_specs=[
            pl.BlockSpec((D_MODEL, tm), lambda t, e: (0, t)),
            pl.BlockSpec((te, D_MODEL), lambda t, e: (e, 0)),
            pl.BlockSpec((D_MODEL, te), lambda t, e: (0, e)),
            per_row, per_row, per_key, per_key,
            pl.BlockSpec((tm, D_MODEL), lambda t, e: (t, 0)),
            pl.BlockSpec((1, D_MODEL), lambda t, e: (0, 0)),
        ],
        out_specs=pl.BlockSpec((tm, D_MODEL), lambda t, e: (t, 0)),
        out_shape=jax.ShapeDtypeStruct((m, D_MODEL), F32),
        scratch_shapes=[pltpu.VMEM((D_MODEL, tm), F32)],
        compiler_params=pltpu.CompilerParams(
            dimension_semantics=("parallel", "arbitrary"), vmem_limit_bytes=VMEM_LIMIT),
        name="peer_experts",
    )(xnt, u_bf16, vt_bf16, cnt, e1, rk2, e2, x1, final_g.reshape(1, D_MODEL))


def kernel(x, mix_norm_g, w_in, conv_w, conv_b, attn_out_g, conv_out_g, w_out, ffn_norm_g,
           peer_wq, peer_keys, peer_u, peer_v, final_norm_g):
    b, s, d = x.shape
    assert w_in.shape[0] == 1, "single-layer block"
    w_in_b = w_in[0].astype(BF16)
    w_out_b = w_out[0].astype(BF16)
    wq_b = peer_wq[0].astype(BF16)
    keys_b = peer_keys[0].astype(BF16)
    u_b = peer_u[0].astype(BF16)
    vt_b = peer_v[0].T.astype(BF16)
    m = b * s
    x2 = x.reshape(m, d)
    qkv, cv = _inproj(x2, mix_norm_g[0], w_in_b)
    oa = _attention(qkv.reshape(b, s, 3 * SB_WIDTH)).reshape(m, SB_WIDTH)
    x1, xnt, cnt, e1, rk2, e2 = _mix_route(
        x2, oa, cv, conv_w[0], conv_b[0], attn_out_g[0], conv_out_g[0],
        w_out_b, ffn_norm_g[0], wq_b, keys_b, s)
    out = _peer_experts(xnt, u_b, vt_b, cnt, e1, rk2, e2, x1, final_norm_g)
    return out.reshape(b, s, d)
```

```python
import functools
import math

import jax
import jax.numpy as jnp
from jax import lax
from jax.experimental import pallas as pl
from jax.experimental.pallas import tpu as pltpu

F32 = jnp.float32
BF16 = jnp.bfloat16

D_MODEL = 1024
SB_HEADS = 8
HEAD_DIM = 64
SB_WIDTH = SB_HEADS * HEAD_DIM
CONV_WIDTH = D_MODEL - SB_WIDTH
CONV_K = 3
N_KEYS = 128
PEER_HEADS = 8
PEER_DK = 256
D_HALF = PEER_DK // 2
PEER_TOPK = 16
N_PICKS = PEER_HEADS * PEER_TOPK
EPS = 1e-6
GELU_C0 = math.sqrt(2.0 / math.pi)
GELU_C1 = 0.044715

SUBLANES = 8
LANES = 128
BF16_ROWS = 16
ROW_TILES = D_MODEL // LANES

TM_PROJ = 512
TQ = 512
TK = 256
TM_MIX = 512
TM_PEER = 512
TE_PEER = 2048
VMEM_LIMIT = 56 * 1024 * 1024


def _rms(x, g):
    return x * lax.rsqrt(jnp.mean(x * x, axis=-1, keepdims=True) + EPS) * g


def _inproj_kernel(x_ref, g_ref, w_ref, qkv_ref, cv_ref):
    h = _rms(x_ref[...], g_ref[...])
    p = jnp.dot(h.astype(BF16), w_ref[...], preferred_element_type=F32)
    qkv_ref[...] = p[:, : 3 * SB_WIDTH].astype(BF16)
    cv_ref[...] = p[:, 3 * SB_WIDTH:]


def _inproj(x2, g, w_bf16):
    m = x2.shape[0]
    n = w_bf16.shape[1]
    return pl.pallas_call(
        _inproj_kernel,
        grid=(m // TM_PROJ,),
        in_specs=[
            pl.BlockSpec((TM_PROJ, D_MODEL), lambda i: (i, 0)),
            pl.BlockSpec((1, D_MODEL), lambda i: (0, 0)),
            pl.BlockSpec((D_MODEL, n), lambda i: (0, 0)),
        ],
        out_specs=[
            pl.BlockSpec((TM_PROJ, 3 * SB_WIDTH), lambda i: (i, 0)),
            pl.BlockSpec((TM_PROJ, 3 * CONV_WIDTH), lambda i: (i, 0)),
        ],
        out_shape=[
            jax.ShapeDtypeStruct((m, 3 * SB_WIDTH), BF16),
            jax.ShapeDtypeStruct((m, 3 * CONV_WIDTH), F32),
        ],
        compiler_params=pltpu.CompilerParams(
            dimension_semantics=("parallel",), vmem_limit_bytes=VMEM_LIMIT),
        name="inproj",
    )(x2, g.reshape(1, D_MODEL), w_bf16)


def _softplus(z):
    return jnp.maximum(z, 0.0) + jnp.log(1.0 + jnp.exp(-jnp.abs(z)))


def _attn_kernel(q_ref, k_ref, v_ref, o_ref):
    i = pl.program_id(2)
    scale = 1.0 / math.sqrt(HEAD_DIM)
    kr = lax.broadcasted_iota(jnp.int32, (TK, TK), 0)
    kc = lax.broadcasted_iota(jnp.int32, (TK, TK), 1)
    tri = (kr >= kc).astype(BF16)
    rows = lax.broadcasted_iota(jnp.int32, (TQ, TK), 0)
    cols = lax.broadcasted_iota(jnp.int32, (TQ, TK), 1)
    kb = TQ // TK
    heads = [slice(hh * HEAD_DIM, (hh + 1) * HEAD_DIM) for hh in range(2)]
    qs = [q_ref[:, sl] * scale for sl in heads]

    def block(q, s0, sl, o, r, mask):
        k = k_ref[pl.ds(s0, TK), sl]
        v = v_ref[pl.ds(s0, TK), sl]
        z = lax.dot_general(q, k, (((1,), (1,)), ((), ())), preferred_element_type=F32)
        l1m = -_softplus(z)
        if mask is not None:
            l1m = jnp.where(mask, l1m, 0.0)
        l_hi = l1m.astype(BF16)
        l_lo = (l1m - l_hi.astype(F32)).astype(BF16)
        c = (jnp.dot(l_hi, tri, preferred_element_type=F32)
             + jnp.dot(l_lo, tri, preferred_element_type=F32) + r)
        a = jnp.exp(z + c)
        if mask is not None:
            a = jnp.where(mask, a, 0.0)
        o = o + jnp.dot(a.astype(BF16), v, preferred_element_type=F32)
        return o, c[:, 0:1]

    carry = tuple((jnp.zeros((TQ, HEAD_DIM), F32), jnp.zeros((TQ, 1), F32)) for _ in range(2))
    for d in reversed(range(kb)):
        s0 = pl.multiple_of(i * TQ + d * TK, TK)
        mask = rows > cols + d * TK
        carry = tuple(block(qs[hh], s0, heads[hh], carry[hh][0], carry[hh][1], mask)
                      for hh in range(2))

    def body(jj, carry):
        s0 = pl.multiple_of((i * kb - 1 - jj) * TK, TK)
        return tuple(block(qs[hh], s0, heads[hh], carry[hh][0], carry[hh][1], None)
                     for hh in range(2))

    carry = lax.fori_loop(0, i * kb, body, carry)
    o_ref[...] = jnp.concatenate([carry[0][0], carry[1][0]], axis=1)


def _attention(qkv3):
    b, s, _ = qkv3.shape
    hp = SB_HEADS // 2
    return pl.pallas_call(
        _attn_kernel,
        grid=(b, hp, s // TQ),
        in_specs=[
            pl.BlockSpec((None, TQ, LANES), lambda bi, h, i: (bi, i, h)),
            pl.BlockSpec((None, s, LANES), lambda bi, h, i: (bi, 0, hp + h)),
            pl.BlockSpec((None, s, LANES), lambda bi, h, i: (bi, 0, 2 * hp + h)),
        ],
        out_specs=pl.BlockSpec((None, TQ, LANES), lambda bi, h, i: (bi, i, h)),
        out_shape=jax.ShapeDtypeStruct((b, s, SB_WIDTH), F32),
        compiler_params=pltpu.CompilerParams(
            dimension_semantics=("parallel", "parallel", "arbitrary"),
            vmem_limit_bytes=VMEM_LIMIT),
        name="sb_attention",
    )(qkv3, qkv3, qkv3)


def _topk_rows(s, k, want_rank):
    n = s.shape[0]
    iota = lax.broadcasted_iota(jnp.int32, s.shape, 0).astype(F32)
    rank = jnp.full(s.shape, float(k), F32) if want_rank else None
    vals, idxs = [], []
    for r in range(k):
        m = jnp.max(s, axis=0, keepdims=True)
        im = jnp.min(jnp.where(s == m, iota, float(n)), axis=0, keepdims=True)
        hit = iota == im
        vals.append(m)
        idxs.append(im)
        if want_rank:
            rank = jnp.where(hit, float(r), rank)
        s = jnp.where(hit, -jnp.inf, s)
    return jnp.concatenate(vals, axis=0), jnp.concatenate(idxs, axis=0), rank


def _mix_route_kernel(tiles_per_seq, x_ref, oa_ref, gb_ref, gc_ref, hc_ref, gcp_ref, hcp_ref,
                      cw_ref, cb_ref, ag_ref, cg_ref, wo_ref, fg_ref, wq_ref, keys_ref,
                      x1_ref, xnt_ref, cnt_ref, e1_ref, rk2_ref, e2_ref, q_sc):
    i = pl.program_id(0)
    tm = x_ref.shape[0]
    u = gc_ref[...] * hc_ref[...]
    first = (i % tiles_per_seq) == 0
    u_prev = jnp.where(first, 0.0, gcp_ref[...] * hcp_ref[...])
    u_ext = jnp.concatenate([u_prev, u], axis=0)
    cw = cw_ref[...]
    y = (cw[0:1, :] * u_ext[SUBLANES - 2:SUBLANES - 2 + tm, :]
         + cw[1:2, :] * u_ext[SUBLANES - 1:SUBLANES - 1 + tm, :]
         + cw[2:3, :] * u) + cb_ref[...]
    o_conv = gb_ref[...] * y
    mixed = jnp.concatenate([_rms(oa_ref[...], ag_ref[...]), _rms(o_conv, cg_ref[...])], axis=1)
    x1 = x_ref[...] + jnp.dot(mixed.astype(BF16), wo_ref[...], preferred_element_type=F32)
    x1_ref[...] = x1
    xn = _rms(x1, fg_ref[...]).astype(BF16)
    xnt_ref[...] = xn.T
    q_sc[...] = jnp.dot(xn, wq_ref[...], preferred_element_type=F32)

    def head(h, carry):
        c0 = pl.multiple_of(h * PEER_DK, PEER_DK)
        q1 = q_sc[:, pl.ds(c0, D_HALF)].astype(BF16)
        q2 = q_sc[:, pl.ds(c0 + D_HALF, D_HALF)].astype(BF16)
        nt = (((1,), (1,)), ((), ()))
        s1 = lax.dot_general(keys_ref[0, h], q1, nt, preferred_element_type=F32)
        s2 = lax.dot_general(keys_ref[1, h], q2, nt, preferred_element_type=F32)
        v1, i1, _ = _topk_rows(s1, PEER_TOPK, False)
        v2, _, rk2 = _topk_rows(s2, PEER_TOPK, True)
        widths = [PEER_TOPK // (a + 1) for a in range(PEER_TOPK)]
        n_cand = sum(widths)
        pad = -n_cand % SUBLANES
        cand = jnp.concatenate(
            [v1[a:a + 1, :] + v2[0:widths[a], :] for a in range(PEER_TOPK)]
            + [jnp.full((pad, tm), -jnp.inf, F32)], axis=0)
        iota = lax.broadcasted_iota(jnp.int32, cand.shape, 0).astype(F32)
        left = cand
        z = jnp.zeros((1, tm), F32)
        for r in range(PEER_TOPK):
            m = jnp.max(left, axis=0, keepdims=True)
            if r == 0:
                m0 = m
            im = jnp.min(jnp.where(left == m, iota, float(n_cand + pad)), axis=0, keepdims=True)
            left = jnp.where(iota == im, -jnp.inf, left)
            z = z + jnp.exp(m - m0)
        picked = jnp.where(left == cand, 0.0, 1.0)
        cnt = jnp.zeros_like(s1)
        key_row = lax.broadcasted_iota(jnp.int32, s1.shape, 0).astype(F32)
        off = 0
        for a in range(PEER_TOPK):
            n_a = jnp.sum(picked[off:off + widths[a], :], axis=0, keepdims=True)
            cnt = jnp.where(key_row == i1[a:a + 1, :], n_a, cnt)
            off += widths[a]
        cnt_ref[h] = cnt
        e1_ref[h] = jnp.exp(s1 - v1[0:1, :])
        rk2_ref[h] = rk2.astype(BF16)
        e2_ref[h] = (jnp.exp(s2 - v2[0:1, :]) * (0.5 / z)).astype(BF16)
        return carry

    lax.fori_loop(0, PEER_HEADS, head, 0)


def _mix_route(x2, oa, cv, conv_w, conv_b, attn_g, conv_g, wo_bf16, ffn_g, wq_bf16, keys_bf16, seq):
    m = x2.shape[0]
    tm = TM_MIX
    row = lambda i: (i, 0)
    const2 = lambda i: (0, 0)
    halo = tm // SUBLANES
    return pl.pallas_call(
        functools.partial(_mix_route_kernel, seq // tm),
        grid=(m // tm,),
        in_specs=[
            pl.BlockSpec((tm, D_MODEL), row),
            pl.BlockSpec((tm, SB_WIDTH), row),
            pl.BlockSpec((tm, CONV_WIDTH), lambda i: (i, 0)),
            pl.BlockSpec((tm, CONV_WIDTH), lambda i: (i, 1)),
            pl.BlockSpec((tm, CONV_WIDTH), lambda i: (i, 2)),
            pl.BlockSpec((SUBLANES, CONV_WIDTH), lambda i: (jnp.maximum(i * halo - 1, 0), 1)),
            pl.BlockSpec((SUBLANES, CONV_WIDTH), lambda i: (jnp.maximum(i * halo - 1, 0), 2)),
            pl.BlockSpec((CONV_K, CONV_WIDTH), const2),
            pl.BlockSpec((1, CONV_WIDTH), const2),
            pl.BlockSpec((1, SB_WIDTH), const2),
            pl.BlockSpec((1, CONV_WIDTH), const2),
            pl.BlockSpec((D_MODEL, D_MODEL), const2),
            pl.BlockSpec((1, D_MODEL), const2),
            pl.BlockSpec((D_MODEL, PEER_HEADS * PEER_DK), const2),
            pl.BlockSpec((2, PEER_HEADS, N_KEYS, D_HALF), lambda i: (0, 0, 0, 0)),
        ],
        out_specs=[
            pl.BlockSpec((tm, D_MODEL), row),
            pl.BlockSpec((D_MODEL, tm), lambda i: (0, i)),
        ] + [pl.BlockSpec((PEER_HEADS, N_KEYS, tm), lambda i: (0, 0, i))] * 4,
        out_shape=[
            jax.ShapeDtypeStruct((m, D_MODEL), F32),
            jax.ShapeDtypeStruct((D_MODEL, m), BF16),
        ] + [jax.ShapeDtypeStruct((PEER_HEADS, N_KEYS, m), dt) for dt in (F32, F32, BF16, BF16)],
        scratch_shapes=[pltpu.VMEM((tm, PEER_HEADS * PEER_DK), F32)],
        compiler_params=pltpu.CompilerParams(
            dimension_semantics=("parallel",), vmem_limit_bytes=VMEM_LIMIT),
        name="mix_route",
    )(x2, oa, cv, cv, cv, cv, cv, conv_w, conv_b.reshape(1, -1), attn_g.reshape(1, -1),
      conv_g.reshape(1, -1), wo_bf16, ffn_g.reshape(1, -1), wq_bf16, keys_bf16)


def _peer_kernel(xnt_ref, u_ref, vt_ref, cnt_ref, e1_ref, rk2_ref, e2_ref, x1_ref, fg_ref,
                 out_ref, acc_sc):
    e = pl.program_id(1)

    @pl.when(e == 0)
    def _():
        acc_sc[...] = jnp.zeros_like(acc_sc)

    xnt = xnt_ref[...]
    rows = TE_PEER // N_KEYS
    w_parts = []
    for r in range(rows):
        sl = slice(r * N_KEYS, (r + 1) * N_KEYS)
        act = jnp.dot(u_ref[sl, :], xnt, preferred_element_type=F32)
        tiles = (N_KEYS // BF16_ROWS, BF16_ROWS, act.shape[1])
        gate = jnp.zeros(tiles, BF16)
        for h in range(PEER_HEADS):
            cnt_t = jnp.broadcast_to(cnt_ref[h, r:r + 1, :], tiles[1:]).astype(BF16)
            e1_t = jnp.broadcast_to(e1_ref[h, r:r + 1, :], tiles[1:]).astype(BF16)
            picked = rk2_ref[h].reshape(tiles) < cnt_t[None]
            gate = gate + jnp.where(picked, e2_ref[h].reshape(tiles) * e1_t[None], 0.0)
        inner = act * (GELU_C0 + (GELU_C0 * GELU_C1) * (act * act))
        w_parts.append((act * (1.0 + jnp.tanh(inner))).astype(BF16) * gate.reshape(act.shape))
    w = jnp.concatenate(w_parts, axis=0)
    acc_sc[...] += jnp.dot(vt_ref[...], w, preferred_element_type=F32)

    @pl.when(e == pl.num_programs(1) - 1)
    def _():
        out_ref[...] = _rms(x1_ref[...] + acc_sc[...].T, fg_ref[...])


def _peer_experts(xnt, u_bf16, vt_bf16, cnt, e1, rk2, e2, x1, final_g):
    m = xnt.shape[1]
    n_exp = u_bf16.shape[0]
    tm, te = TM_PEER, TE_PEER
    rows = te // N_KEYS
    per_row = pl.BlockSpec((PEER_HEADS, rows, tm), lambda t, e: (0, e, t))
    per_key = pl.BlockSpec((PEER_HEADS, N_KEYS, tm), lambda t, e: (0, 0, t))
    return pl.pallas_call(
        _peer_kernel,
        grid=(m // tm, n_exp // te),
        in_specs=[
            pl.BlockSpec((D_MODEL, tm), lambda t, e: (0, t)),
            pl.BlockSpec((te, D_MODEL), lambda t, e: (e, 0)),
            pl.BlockSpec((D_MODEL, te), lambda t, e: (0, e)),
            per_row, per_row, per_key, per_key,
            pl.BlockSpec((tm, D_MODEL), lambda t, e: (t, 0)),
            pl.BlockSpec((1, D_MODEL), lambda t, e: (0, 0)),
        ],
        out_specs=pl.BlockSpec((tm, D_MODEL), lambda t, e: (t, 0)),
        out_shape=jax.ShapeDtypeStruct((m, D_MODEL), F32),
        scratch_shapes=[pltpu.VMEM((D_MODEL, tm), F32)],
        compiler_params=pltpu.CompilerParams(
            dimension_semantics=("parallel", "arbitrary"), vmem_limit_bytes=VMEM_LIMIT),
        name="peer_experts",
    )(xnt, u_bf16, vt_bf16, cnt, e1, rk2, e2, x1, final_g.reshape(1, D_MODEL))


def kernel(x, mix_norm_g, w_in, conv_w, conv_b, attn_out_g, conv_out_g, w_out, ffn_norm_g,
           peer_wq, peer_keys, peer_u, peer_v, final_norm_g):
    b, s, d = x.shape
    assert w_in.shape[0] == 1, "single-layer block"
    w_in_b = w_in[0].astype(BF16)
    w_out_b = w_out[0].astype(BF16)
    wq_b = peer_wq[0].astype(BF16)
    keys_b = peer_keys[0].astype(BF16)
    u_b = peer_u[0].astype(BF16)
    vt_b = peer_v[0].T.astype(BF16)
    m = b * s
    x2 = x.reshape(m, d)
    qkv, cv = _inproj(x2, mix_norm_g[0], w_in_b)
    oa = _attention(qkv.reshape(b, s, 3 * SB_WIDTH)).reshape(m, SB_WIDTH)
    x1, xnt, cnt, e1, rk2, e2 = _mix_route(
        x2, oa, cv, conv_w[0], conv_b[0], attn_out_g[0], conv_out_g[0],
        w_out_b, ffn_norm_g[0], wq_b, keys_b, s)
    out = _peer_experts(xnt, u_b, vt_b, cnt, e1, rk2, e2, x1, final_norm_g)
    return out.reshape(b, s, d)
```

```python
import functools
import math

import jax
import jax.numpy as jnp
from jax import lax
from jax.experimental import pallas as pl
from jax.experimental.pallas import tpu as pltpu

F32 = jnp.float32
BF16 = jnp.bfloat16

D_MODEL = 1024
SB_HEADS = 8
HEAD_DIM = 64
SB_WIDTH = SB_HEADS * HEAD_DIM
CONV_WIDTH = D_MODEL - SB_WIDTH
CONV_K = 3
N_KEYS = 128
PEER_HEADS = 8
PEER_DK = 256
D_HALF = PEER_DK // 2
PEER_TOPK = 16
N_PICKS = PEER_HEADS * PEER_TOPK
EPS = 1e-6
GELU_C0 = math.sqrt(2.0 / math.pi)
GELU_C1 = 0.044715

SUBLANES = 8
LANES = 128
BF16_ROWS = 16
ROW_TILES = D_MODEL // LANES

TM_PROJ = 512
TQ = 512
TK = 256
TM_MIX = 512
TM_PEER = 512
TE_PEER = 2048
VMEM_LIMIT = 56 * 1024 * 1024


def _rms(x, g):
    return x * lax.rsqrt(jnp.mean(x * x, axis=-1, keepdims=True) + EPS) * g


def _inproj_kernel(x_ref, g_ref, w_ref, qkv_ref, cv_ref):
    h = _rms(x_ref[...], g_ref[...])
    p = jnp.dot(h.astype(BF16), w_ref[...], preferred_element_type=F32)
    qkv_ref[...] = p[:, : 3 * SB_WIDTH].astype(BF16)
    cv_ref[...] = p[:, 3 * SB_WIDTH:]


def _inproj(x2, g, w_bf16):
    m = x2.shape[0]
    n = w_bf16.shape[1]
    return pl.pallas_call(
        _inproj_kernel,
        grid=(m // TM_PROJ,),
        in_specs=[
            pl.BlockSpec((TM_PROJ, D_MODEL), lambda i: (i, 0)),
            pl.BlockSpec((1, D_MODEL), lambda i: (0, 0)),
            pl.BlockSpec((D_MODEL, n), lambda i: (0, 0)),
        ],
        out_specs=[
            pl.BlockSpec((TM_PROJ, 3 * SB_WIDTH), lambda i: (i, 0)),
            pl.BlockSpec((TM_PROJ, 3 * CONV_WIDTH), lambda i: (i, 0)),
        ],
        out_shape=[
            jax.ShapeDtypeStruct((m, 3 * SB_WIDTH), BF16),
            jax.ShapeDtypeStruct((m, 3 * CONV_WIDTH), F32),
        ],
        compiler_params=pltpu.CompilerParams(
            dimension_semantics=("parallel",), vmem_limit_bytes=VMEM_LIMIT),
        name="inproj",
    )(x2, g.reshape(1, D_MODEL), w_bf16)


def _softplus(z):
    return jnp.maximum(z, 0.0) + jnp.log(1.0 + jnp.exp(-jnp.abs(z)))


def _attn_kernel(q_ref, k_ref, v_ref, o_ref):
    i = pl.program_id(2)
    scale = 1.0 / math.sqrt(HEAD_DIM)
    kr = lax.broadcasted_iota(jnp.int32, (TK, TK), 0)
    kc = lax.broadcasted_iota(jnp.int32, (TK, TK), 1)
    tri = (kr >= kc).astype(BF16)
    rows = lax.broadcasted_iota(jnp.int32, (TQ, TK), 0)
    cols = lax.broadcasted_iota(jnp.int32, (TQ, TK), 1)
    kb = TQ // TK
    heads = [slice(hh * HEAD_DIM, (hh + 1) * HEAD_DIM) for hh in range(2)]
    qs = [q_ref[:, sl] * scale for sl in heads]

    def block(q, s0, sl, o, r, mask):
        k = k_ref[pl.ds(s0, TK), sl]
        v = v_ref[pl.ds(s0, TK), sl]
        z = lax.dot_general(q, k, (((1,), (1,)), ((), ())), preferred_element_type=F32)
        l1m = -_softplus(z)
        if mask is not None:
            l1m = jnp.where(mask, l1m, 0.0)
        l_hi = l1m.astype(BF16)
        l_lo = (l1m - l_hi.astype(F32)).astype(BF16)
        c = (jnp.dot(l_hi, tri, preferred_element_type=F32)
             + jnp.dot(l_lo, tri, preferred_element_type=F32) + r)
        a = jnp.exp(z + c)
        if mask is not None:
            a = jnp.where(mask, a, 0.0)
        o = o + jnp.dot(a.astype(BF16), v, preferred_element_type=F32)
        return o, c[:, 0:1]

    carry = tuple((jnp.zeros((TQ, HEAD_DIM), F32), jnp.zeros((TQ, 1), F32)) for _ in range(2))
    for d in reversed(range(kb)):
        s0 = pl.multiple_of(i * TQ + d * TK, TK)
        mask = rows > cols + d * TK
        carry = tuple(block(qs[hh], s0, heads[hh], carry[hh][0], carry[hh][1], mask)
                      for hh in range(2))

    def body(jj, carry):
        s0 = pl.multiple_of((i * kb - 1 - jj) * TK, TK)
        return tuple(block(qs[hh], s0, heads[hh], carry[hh][0], carry[hh][1], None)
                     for hh in range(2))

    carry = lax.fori_loop(0, i * kb, body, carry)
    o_ref[...] = jnp.concatenate([carry[0][0], carry[1][0]], axis=1)


def _attention(qkv3):
    b, s, _ = qkv3.shape
    hp = SB_HEADS // 2
    return pl.pallas_call(
        _attn_kernel,
        grid=(b, hp, s // TQ),
        in_specs=[
            pl.BlockSpec((None, TQ, LANES), lambda bi, h, i: (bi, i, h)),
            pl.BlockSpec((None, s, LANES), lambda bi, h, i: (bi, 0, hp + h)),
            pl.BlockSpec((None, s, LANES), lambda bi, h, i: (bi, 0, 2 * hp + h)),
        ],
        out_specs=pl.BlockSpec((None, TQ, LANES), lambda bi, h, i: (bi, i, h)),
        out_shape=jax.ShapeDtypeStruct((b, s, SB_WIDTH), F32),
        compiler_params=pltpu.CompilerParams(
            dimension_semantics=("parallel", "parallel", "arbitrary"),
            vmem_limit_bytes=VMEM_LIMIT),
        name="sb_attention",
    )(qkv3, qkv3, qkv3)


def _extract_topk(s, k, want_rank, exact):
    n = s.shape[0]
    s0 = s
    iota = lax.broadcasted_iota(jnp.int32, s.shape, 0).astype(F32) if exact else None
    rank = jnp.full(s.shape, float(k), F32) if want_rank else None
    vals, idxs = [], []
    for r in range(k):
        m = jnp.max(s, axis=0, keepdims=True)
        if exact:
            im = jnp.min(jnp.where(s == m, iota, float(n)), axis=0, keepdims=True)
            hit = iota == im
            idxs.append(im)
        else:
            hit = s == m
        vals.append(m)
        if want_rank:
            rank = jnp.where(hit, float(r), rank)
        s = jnp.where(hit, -jnp.inf, s)
    taken = jnp.sum(jnp.where(s == s0, 0.0, 1.0), axis=0, keepdims=True)
    rows = jnp.concatenate(idxs, axis=0) if exact else None
    return jnp.concatenate(vals, axis=0), rows, rank, s, taken


def _mix_route_kernel(tiles_per_seq, x_ref, oa_ref, gb_ref, gc_ref, hc_ref, gcp_ref, hcp_ref,
                      cw_ref, cb_ref, ag_ref, cg_ref, wo_ref, fg_ref, wq_ref, keys_ref,
                      x1_ref, xnt_ref, cnt_ref, e1_ref, rk2_ref, e2_ref, q_sc):
    i = pl.program_id(0)
    tm = x_ref.shape[0]
    u = gc_ref[...] * hc_ref[...]
    first = (i % tiles_per_seq) == 0
    u_prev = jnp.where(first, 0.0, gcp_ref[...] * hcp_ref[...])
    u_ext = jnp.concatenate([u_prev, u], axis=0)
    cw = cw_ref[...]
    y = (cw[0:1, :] * u_ext[SUBLANES - 2:SUBLANES - 2 + tm, :]
         + cw[1:2, :] * u_ext[SUBLANES - 1:SUBLANES - 1 + tm, :]
         + cw[2:3, :] * u) + cb_ref[...]
    o_conv = gb_ref[...] * y
    mixed = jnp.concatenate([_rms(oa_ref[...], ag_ref[...]), _rms(o_conv, cg_ref[...])], axis=1)
    x1 = x_ref[...] + jnp.dot(mixed.astype(BF16), wo_ref[...], preferred_element_type=F32)
    x1_ref[...] = x1
    xn = _rms(x1, fg_ref[...]).astype(BF16)
    xnt_ref[...] = xn.T
    q_sc[...] = jnp.dot(xn, wq_ref[...], preferred_element_type=F32)

    def head(h, carry):
        c0 = pl.multiple_of(h * PEER_DK, PEER_DK)
        q1 = q_sc[:, pl.ds(c0, D_HALF)].astype(BF16)
        q2 = q_sc[:, pl.ds(c0 + D_HALF, D_HALF)].astype(BF16)
        nt = (((1,), (1,)), ((), ()))
        s1 = lax.dot_general(keys_ref[0, h], q1, nt, preferred_element_type=F32)
        s2 = lax.dot_general(keys_ref[1, h], q2, nt, preferred_element_type=F32)

        def route(exact):
            v1, i1, _, _, took1 = _extract_topk(s1, PEER_TOPK, False, exact)
            v2, _, rk2, _, took2 = _extract_topk(s2, PEER_TOPK, True, exact)
            widths = [PEER_TOPK // (a + 1) for a in range(PEER_TOPK)]
            pad = -sum(widths) % SUBLANES
            cand = jnp.concatenate(
                [v1[a:a + 1, :] + v2[0:widths[a], :] for a in range(PEER_TOPK)]
                + [jnp.full((pad, tm), -jnp.inf, F32)], axis=0)
            best, _, _, left, took = _extract_topk(cand, PEER_TOPK, False, exact)
            picked = jnp.where(left == cand, 0.0, 1.0)
            z = jnp.sum(jnp.exp(best - best[0:1, :]), axis=0, keepdims=True)
            cnt = jnp.zeros_like(s1)
            key_row = lax.broadcasted_iota(jnp.int32, s1.shape, 0).astype(F32) if exact else None
            off = 0
            for a in range(PEER_TOPK):
                n_a = jnp.sum(picked[off:off + widths[a], :], axis=0, keepdims=True)
                at_a = key_row == i1[a:a + 1, :] if exact else s1 == v1[a:a + 1, :]
                cnt = jnp.where(at_a, n_a, cnt)
                off += widths[a]
            cnt_ref[h] = cnt
            e1_ref[h] = jnp.exp(s1 - v1[0:1, :])
            rk2_ref[h] = rk2.astype(BF16)
            e2_ref[h] = (jnp.exp(s2 - v2[0:1, :]) * (0.5 / z)).astype(BF16)
            k = float(PEER_TOPK)
            return jnp.where((took1 == k) & (took2 == k) & (took == k), 0.0, 1.0)

        tied = route(exact=False)

        @pl.when(jnp.max(tied) > 0.0)
        def _():
            route(exact=True)

        return carry

    lax.fori_loop(0, PEER_HEADS, head, 0)


def _mix_route(x2, oa, cv, conv_w, conv_b, attn_g, conv_g, wo_bf16, ffn_g, wq_bf16, keys_bf16, seq):
    m = x2.shape[0]
    tm = TM_MIX
    row = lambda i: (i, 0)
    const2 = lambda i: (0, 0)
    halo = tm // SUBLANES
    return pl.pallas_call(
        functools.partial(_mix_route_kernel, seq // tm),
        grid=(m // tm,),
        in_specs=[
            pl.BlockSpec((tm, D_MODEL), row),
            pl.BlockSpec((tm, SB_WIDTH), row),
            pl.BlockSpec((tm, CONV_WIDTH), lambda i: (i, 0)),
            pl.BlockSpec((tm, CONV_WIDTH), lambda i: (i, 1)),
            pl.BlockSpec((tm, CONV_WIDTH), lambda i: (i, 2)),
            pl.BlockSpec((SUBLANES, CONV_WIDTH), lambda i: (jnp.maximum(i * halo - 1, 0), 1)),
            pl.BlockSpec((SUBLANES, CONV_WIDTH), lambda i: (jnp.maximum(i * halo - 1, 0), 2)),
            pl.BlockSpec((CONV_K, CONV_WIDTH), const2),
            pl.BlockSpec((1, CONV_WIDTH), const2),
            pl.BlockSpec((1, SB_WIDTH), const2),
            pl.BlockSpec((1, CONV_WIDTH), const2),
            pl.BlockSpec((D_MODEL, D_MODEL), const2),
            pl.BlockSpec((1, D_MODEL), const2),
            pl.BlockSpec((D_MODEL, PEER_HEADS * PEER_DK), const2),
            pl.BlockSpec((2, PEER_HEADS, N_KEYS, D_HALF), lambda i: (0, 0, 0, 0)),
        ],
        out_specs=[
            pl.BlockSpec((tm, D_MODEL), row),
            pl.BlockSpec((D_MODEL, tm), lambda i: (0, i)),
        ] + [pl.BlockSpec((PEER_HEADS, N_KEYS, tm), lambda i: (0, 0, i))] * 4,
        out_shape=[
            jax.ShapeDtypeStruct((m, D_MODEL), F32),
            jax.ShapeDtypeStruct((D_MODEL, m), BF16),
        ] + [jax.ShapeDtypeStruct((PEER_HEADS, N_KEYS, m), dt) for dt in (F32, F32, BF16, BF16)],
        scratch_shapes=[pltpu.VMEM((tm, PEER_HEADS * PEER_DK), F32)],
        compiler_params=pltpu.CompilerParams(
            dimension_semantics=("parallel",), vmem_limit_bytes=VMEM_LIMIT),
        name="mix_route",
    )(x2, oa, cv, cv, cv, cv, cv, conv_w, conv_b.reshape(1, -1), attn_g.reshape(1, -1),
      conv_g.reshape(1, -1), wo_bf16, ffn_g.reshape(1, -1), wq_bf16, keys_bf16)


def _peer_kernel(xnt_ref, u_ref, vt_ref, cnt_ref, e1_ref, rk2_ref, e2_ref, x1_ref, fg_ref,
                 out_ref, acc_sc):
    e = pl.program_id(1)

    @pl.when(e == 0)
    def _():
        acc_sc[...] = jnp.zeros_like(acc_sc)

    xnt = xnt_ref[...]
    rows = TE_PEER // N_KEYS
    w_parts = []
    for r in range(rows):
        sl = slice(r * N_KEYS, (r + 1) * N_KEYS)
        act = jnp.dot(u_ref[sl, :], xnt, preferred_element_type=F32)
        tiles = (N_KEYS // BF16_ROWS, BF16_ROWS, act.shape[1])
        gate = jnp.zeros(tiles, BF16)
        for h in range(PEER_HEADS):
            cnt_t = jnp.broadcast_to(cnt_ref[h, r:r + 1, :], tiles[1:]).astype(BF16)
            e1_t = jnp.broadcast_to(e1_ref[h, r:r + 1, :], tiles[1:]).astype(BF16)
            picked = rk2_ref[h].reshape(tiles) < cnt_t[None]
            gate = gate + jnp.where(picked, e2_ref[h].reshape(tiles) * e1_t[None], 0.0)
        inner = act * (GELU_C0 + (GELU_C0 * GELU_C1) * (act * act))
        w_parts.append((act * (1.0 + jnp.tanh(inner))).astype(BF16) * gate.reshape(act.shape))
    w = jnp.concatenate(w_parts, axis=0)
    acc_sc[...] += jnp.dot(vt_ref[...], w, preferred_element_type=F32)

    @pl.when(e == pl.num_programs(1) - 1)
    def _():
        out_ref[...] = _rms(x1_ref[...] + acc_sc[...].T, fg_ref[...])


def _peer_experts(xnt, u_bf16, vt_bf16, cnt, e1, rk2, e2, x1, final_g):
    m = xnt.shape[1]
    n_exp = u_bf16.shape[0]
    tm, te = TM_PEER, TE_PEER
    rows = te // N_KEYS
    per_row = pl.BlockSpec((PEER_HEADS, rows, tm), lambda t, e: (0, e, t))
    per_key = pl.BlockSpec((PEER_HEADS, N_KEYS, tm), lambda t, e: (0, 0, t))
    return pl.pallas_call(
        _peer_kernel,
        grid=(m // tm, n_exp // te),
        in_specs=[
            pl.BlockSpec((D_MODEL, tm), lambda t, e: (0, t)),
            pl.BlockSpec((te, D_MODEL), lambda t, e: (e, 0)),
            pl.BlockSpec((D_MODEL, te), lambda t, e: (0, e)),
            per_row, per_row, per_key, per_key,
            pl.BlockSpec((tm, D_MODEL), lambda t, e: (t, 0)),
            pl.BlockSpec((1, D_MODEL), lambda t, e: (0, 0)),
        ],
        out_specs=pl.BlockSpec((tm, D_MODEL), lambda t, e: (t, 0)),
        out_shape=jax.ShapeDtypeStruct((m, D_MODEL), F32),
        scratch_shapes=[pltpu.VMEM((D_MODEL, tm), F32)],
        compiler_params=pltpu.CompilerParams(
            dimension_semantics=("parallel", "arbitrary"), vmem_limit_bytes=VMEM_LIMIT),
        name="peer_experts",
    )(xnt, u_bf16, vt_bf16, cnt, e1, rk2, e2, x1, final_g.reshape(1, D_MODEL))


def kernel(x, mix_norm_g, w_in, conv_w, conv_b, attn_out_g, conv_out_g, w_out, ffn_norm_g,
           peer_wq, peer_keys, peer_u, peer_v, final_norm_g):
    b, s, d = x.shape
    assert w_in.shape[0] == 1, "single-layer block"
    w_in_b = w_in[0].astype(BF16)
    w_out_b = w_out[0].astype(BF16)
    wq_b = peer_wq[0].astype(BF16)
    keys_b = peer_keys[0].astype(BF16)
    u_b = peer_u[0].astype(BF16)
    vt_b = peer_v[0].T.astype(BF16)
    m = b * s
    x2 = x.reshape(m, d)
    qkv, cv = _inproj(x2, mix_norm_g[0], w_in_b)
    oa = _attention(qkv.reshape(b, s, 3 * SB_WIDTH)).reshape(m, SB_WIDTH)
    x1, xnt, cnt, e1, rk2, e2 = _mix_route(
        x2, oa, cv, conv_w[0], conv_b[0], attn_out_g[0], conv_out_g[0],
        w_out_b, ffn_norm_g[0], wq_b, keys_b, s)
    out = _peer_experts(xnt, u_b, vt_b, cnt, e1, rk2, e2, x1, final_norm_g)
    return out.reshape(b, s, d)
```

```python
import functools
import math

import jax
import jax.numpy as jnp
from jax import lax
from jax.experimental import pallas as pl
from jax.experimental.pallas import tpu as pltpu

F32 = jnp.float32
BF16 = jnp.bfloat16

D_MODEL = 1024
SB_HEADS = 8
HEAD_DIM = 64
SB_WIDTH = SB_HEADS * HEAD_DIM
CONV_WIDTH = D_MODEL - SB_WIDTH
CONV_K = 3
N_KEYS = 128
PEER_HEADS = 8
PEER_DK = 256
D_HALF = PEER_DK // 2
PEER_TOPK = 16
N_PICKS = PEER_HEADS * PEER_TOPK
EPS = 1e-6
GELU_C0 = math.sqrt(2.0 / math.pi)
GELU_C1 = 0.044715

SUBLANES = 8
LANES = 128
BF16_ROWS = 16
ROW_TILES = D_MODEL // LANES

TM_PROJ = 512
TQ = 512
HEADS_PER_STEP = 4
TK = 256
TM_MIX = 512
TM_PEER = 512
TE_PEER = 2048
VMEM_LIMIT = 56 * 1024 * 1024


def _rms(x, g):
    return x * lax.rsqrt(jnp.mean(x * x, axis=-1, keepdims=True) + EPS) * g


def _inproj_kernel(x_ref, g_ref, w_ref, qkv_ref, cv_ref):
    h = _rms(x_ref[...], g_ref[...])
    p = jnp.dot(h.astype(BF16), w_ref[...], preferred_element_type=F32)
    qkv_ref[...] = p[:, : 3 * SB_WIDTH].astype(BF16)
    cv_ref[...] = p[:, 3 * SB_WIDTH:]


def _inproj(x2, g, w_bf16):
    m = x2.shape[0]
    n = w_bf16.shape[1]
    return pl.pallas_call(
        _inproj_kernel,
        grid=(m // TM_PROJ,),
        in_specs=[
            pl.BlockSpec((TM_PROJ, D_MODEL), lambda i: (i, 0)),
            pl.BlockSpec((1, D_MODEL), lambda i: (0, 0)),
            pl.BlockSpec((D_MODEL, n), lambda i: (0, 0)),
        ],
        out_specs=[
            pl.BlockSpec((TM_PROJ, 3 * SB_WIDTH), lambda i: (i, 0)),
            pl.BlockSpec((TM_PROJ, 3 * CONV_WIDTH), lambda i: (i, 0)),
        ],
        out_shape=[
            jax.ShapeDtypeStruct((m, 3 * SB_WIDTH), BF16),
            jax.ShapeDtypeStruct((m, 3 * CONV_WIDTH), F32),
        ],
        compiler_params=pltpu.CompilerParams(
            dimension_semantics=("parallel",), vmem_limit_bytes=VMEM_LIMIT),
        name="inproj",
    )(x2, g.reshape(1, D_MODEL), w_bf16)


def _softplus(z):
    return jnp.maximum(z, 0.0) + jnp.log(1.0 + jnp.exp(-jnp.abs(z)))


def _attn_kernel(q_ref, k_ref, v_ref, o_ref):
    i = pl.program_id(2)
    scale = 1.0 / math.sqrt(HEAD_DIM)
    kr = lax.broadcasted_iota(jnp.int32, (TK, TK), 0)
    kc = lax.broadcasted_iota(jnp.int32, (TK, TK), 1)
    tri = (kr >= kc).astype(BF16)
    rows = lax.broadcasted_iota(jnp.int32, (TQ, TK), 0)
    cols = lax.broadcasted_iota(jnp.int32, (TQ, TK), 1)
    kb = TQ // TK
    heads = [slice(hh * HEAD_DIM, (hh + 1) * HEAD_DIM) for hh in range(HEADS_PER_STEP)]
    qs = [q_ref[:, sl] * scale for sl in heads]
    nt = (((1,), (1,)), ((), ()))

    def block(s0, carry, mask):
        zs = [lax.dot_general(q, k_ref[pl.ds(s0, TK), sl], nt, preferred_element_type=F32)
              for q, sl in zip(qs, heads)]
        ls = [-_softplus(z) for z in zs]
        if mask is not None:
            ls = [jnp.where(mask, l, 0.0) for l in ls]
        his = [l.astype(BF16) for l in ls]
        los = [(l - hi.astype(F32)).astype(BF16) for l, hi in zip(ls, his)]
        cs = [jnp.dot(hi, tri, preferred_element_type=F32)
              + jnp.dot(lo, tri, preferred_element_type=F32) + r
              for hi, lo, (_, r) in zip(his, los, carry)]
        ps = [jnp.exp(z + c) for z, c in zip(zs, cs)]
        if mask is not None:
            ps = [jnp.where(mask, p, 0.0) for p in ps]
        os = [o + jnp.dot(p.astype(BF16), v_ref[pl.ds(s0, TK), sl], preferred_element_type=F32)
              for p, sl, (o, _) in zip(ps, heads, carry)]
        return tuple((o, c[:, 0:1]) for o, c in zip(os, cs))

    carry = tuple((jnp.zeros((TQ, HEAD_DIM), F32), jnp.zeros((TQ, 1), F32)) for _ in heads)
    for d in reversed(range(kb)):
        carry = block(pl.multiple_of(i * TQ + d * TK, TK), carry, rows > cols + d * TK)

    def body(jj, carry):
        return block(pl.multiple_of((i * kb - 1 - jj) * TK, TK), carry, None)

    carry = lax.fori_loop(0, i * kb, body, carry)
    o_ref[...] = jnp.concatenate([o for o, _ in carry], axis=1)


def _attention(qkv3):
    b, s, _ = qkv3.shape
    hp = SB_HEADS // HEADS_PER_STEP
    hw = HEADS_PER_STEP * HEAD_DIM
    return pl.pallas_call(
        _attn_kernel,
        grid=(b, hp, s // TQ),
        in_specs=[
            pl.BlockSpec((None, TQ, hw), lambda bi, h, i: (bi, i, h)),
            pl.BlockSpec((None, s, hw), lambda bi, h, i: (bi, 0, hp + h)),
            pl.BlockSpec((None, s, hw), lambda bi, h, i: (bi, 0, 2 * hp + h)),
        ],
        out_specs=pl.BlockSpec((None, TQ, hw), lambda bi, h, i: (bi, i, h)),
        out_shape=jax.ShapeDtypeStruct((b, s, SB_WIDTH), F32),
        compiler_params=pltpu.CompilerParams(
            dimension_semantics=("parallel", "parallel", "arbitrary"),
            vmem_limit_bytes=VMEM_LIMIT),
        name="sb_attention",
    )(qkv3, qkv3, qkv3)


def _extract_topk(s, k, want_rank, exact):
    n = s.shape[0]
    s0 = s
    iota = lax.broadcasted_iota(jnp.int32, s.shape, 0).astype(F32) if exact else None
    rank = jnp.full(s.shape, float(k), F32) if want_rank else None
    vals, idxs = [], []
    for r in range(k):
        m = jnp.max(s, axis=0, keepdims=True)
        if exact:
            im = jnp.min(jnp.where(s == m, iota, float(n)), axis=0, keepdims=True)
            hit = iota == im
            idxs.append(im)
        else:
            hit = s == m
        vals.append(m)
        if want_rank:
            rank = jnp.where(hit, float(r), rank)
        s = jnp.where(hit, -jnp.inf, s)
    taken = jnp.sum(jnp.where(s == s0, 0.0, 1.0), axis=0, keepdims=True)
    rows = jnp.concatenate(idxs, axis=0) if exact else None
    return jnp.concatenate(vals, axis=0), rows, rank, s, taken


def _mix_route_kernel(tiles_per_seq, x_ref, oa_ref, gb_ref, gc_ref, hc_ref, gcp_ref, hcp_ref,
                      cw_ref, cb_ref, ag_ref, cg_ref, wo_ref, fg_ref, wq_ref, keys_ref,
                      x1_ref, xnt_ref, cnt_ref, e1_ref, rk2_ref, e2_ref, q_sc):
    i = pl.program_id(0)
    tm = x_ref.shape[0]
    u = gc_ref[...] * hc_ref[...]
    first = (i % tiles_per_seq) == 0
    u_prev = jnp.where(first, 0.0, gcp_ref[...] * hcp_ref[...])
    u_ext = jnp.concatenate([u_prev, u], axis=0)
    cw = cw_ref[...]
    y = (cw[0:1, :] * u_ext[SUBLANES - 2:SUBLANES - 2 + tm, :]
         + cw[1:2, :] * u_ext[SUBLANES - 1:SUBLANES - 1 + tm, :]
         + cw[2:3, :] * u) + cb_ref[...]
    o_conv = gb_ref[...] * y
    mixed = jnp.concatenate([_rms(oa_ref[...], ag_ref[...]), _rms(o_conv, cg_ref[...])], axis=1)
    x1 = x_ref[...] + jnp.dot(mixed.astype(BF16), wo_ref[...], preferred_element_type=F32)
    x1_ref[...] = x1
    xn = _rms(x1, fg_ref[...]).astype(BF16)
    xnt_ref[...] = xn.T
    q_sc[...] = jnp.dot(xn, wq_ref[...], preferred_element_type=F32)

    def head(h, carry):
        c0 = pl.multiple_of(h * PEER_DK, PEER_DK)
        q1 = q_sc[:, pl.ds(c0, D_HALF)].astype(BF16)
        q2 = q_sc[:, pl.ds(c0 + D_HALF, D_HALF)].astype(BF16)
        nt = (((1,), (1,)), ((), ()))
        s1 = lax.dot_general(keys_ref[0, h], q1, nt, preferred_element_type=F32)
        s2 = lax.dot_general(keys_ref[1, h], q2, nt, preferred_element_type=F32)

        def route(exact):
            v1, i1, _, _, took1 = _extract_topk(s1, PEER_TOPK, False, exact)
            v2, _, rk2, _, took2 = _extract_topk(s2, PEER_TOPK, True, exact)
            widths = [PEER_TOPK // (a + 1) for a in range(PEER_TOPK)]
            pad = -sum(widths) % SUBLANES
            cand = jnp.concatenate(
                [v1[a:a + 1, :] + v2[0:widths[a], :] for a in range(PEER_TOPK)]
                + [jnp.full((pad, tm), -jnp.inf, F32)], axis=0)
            best, _, _, left, took = _extract_topk(cand, PEER_TOPK, False, exact)
            picked = jnp.where(left == cand, 0.0, 1.0)
            z = jnp.sum(jnp.exp(best - best[0:1, :]), axis=0, keepdims=True)
            cnt = jnp.zeros_like(s1)
            key_row = lax.broadcasted_iota(jnp.int32, s1.shape, 0).astype(F32) if exact else None
            off = 0
            for a in range(PEER_TOPK):
                n_a = jnp.sum(picked[off:off + widths[a], :], axis=0, keepdims=True)
                at_a = key_row == i1[a:a + 1, :] if exact else s1 == v1[a:a + 1, :]
                cnt = jnp.where(at_a, n_a, cnt)
                off += widths[a]
            cnt_ref[h] = cnt
            e1_ref[h] = jnp.exp(s1 - v1[0:1, :])
            rk2_ref[h] = rk2.astype(BF16)
            e2_ref[h] = (jnp.exp(s2 - v2[0:1, :]) * (0.5 / z)).astype(BF16)
            k = float(PEER_TOPK)
            return jnp.where((took1 == k) & (took2 == k) & (took == k), 0.0, 1.0)

        tied = route(exact=False)

        @pl.when(jnp.max(tied) > 0.0)
        def _():
            route(exact=True)

        return carry

    lax.fori_loop(0, PEER_HEADS, head, 0)


def _mix_route(x2, oa, cv, conv_w, conv_b, attn_g, conv_g, wo_bf16, ffn_g, wq_bf16, keys_bf16, seq):
    m = x2.shape[0]
    tm = TM_MIX
    row = lambda i: (i, 0)
    const2 = lambda i: (0, 0)
    halo = tm // SUBLANES
    return pl.pallas_call(
        functools.partial(_mix_route_kernel, seq // tm),
        grid=(m // tm,),
        in_specs=[
            pl.BlockSpec((tm, D_MODEL), row),
            pl.BlockSpec((tm, SB_WIDTH), row),
            pl.BlockSpec((tm, CONV_WIDTH), lambda i: (i, 0)),
            pl.BlockSpec((tm, CONV_WIDTH), lambda i: (i, 1)),
            pl.BlockSpec((tm, CONV_WIDTH), lambda i: (i, 2)),
            pl.BlockSpec((SUBLANES, CONV_WIDTH), lambda i: (jnp.maximum(i * halo - 1, 0), 1)),
            pl.BlockSpec((SUBLANES, CONV_WIDTH), lambda i: (jnp.maximum(i * halo - 1, 0), 2)),
            pl.BlockSpec((CONV_K, CONV_WIDTH), const2),
            pl.BlockSpec((1, CONV_WIDTH), const2),
            pl.BlockSpec((1, SB_WIDTH), const2),
            pl.BlockSpec((1, CONV_WIDTH), const2),
            pl.BlockSpec((D_MODEL, D_MODEL), const2),
            pl.BlockSpec((1, D_MODEL), const2),
            pl.BlockSpec((D_MODEL, PEER_HEADS * PEER_DK), const2),
            pl.BlockSpec((2, PEER_HEADS, N_KEYS, D_HALF), lambda i: (0, 0, 0, 0)),
        ],
        out_specs=[
            pl.BlockSpec((tm, D_MODEL), row),
            pl.BlockSpec((D_MODEL, tm), lambda i: (0, i)),
        ] + [pl.BlockSpec((PEER_HEADS, N_KEYS, tm), lambda i: (0, 0, i))] * 4,
        out_shape=[
            jax.ShapeDtypeStruct((m, D_MODEL), F32),
            jax.ShapeDtypeStruct((D_MODEL, m), BF16),
        ] + [jax.ShapeDtypeStruct((PEER_HEADS, N_KEYS, m), dt) for dt in (F32, F32, BF16, BF16)],
        scratch_shapes=[pltpu.VMEM((tm, PEER_HEADS * PEER_DK), F32)],
        compiler_params=pltpu.CompilerParams(
            dimension_semantics=("parallel",), vmem_limit_bytes=VMEM_LIMIT),
        name="mix_route",
    )(x2, oa, cv, cv, cv, cv, cv, conv_w, conv_b.reshape(1, -1), attn_g.reshape(1, -1),
      conv_g.reshape(1, -1), wo_bf16, ffn_g.reshape(1, -1), wq_bf16, keys_bf16)


def _peer_kernel(xnt_ref, u_ref, vt_ref, cnt_ref, e1_ref, rk2_ref, e2_ref, x1_ref, fg_ref,
                 out_ref, acc_sc):
    e = pl.program_id(1)

    @pl.when(e == 0)
    def _():
        acc_sc[...] = jnp.zeros_like(acc_sc)

    xnt = xnt_ref[...]
    rows = TE_PEER // N_KEYS
    w_parts = []
    for r in range(rows):
        sl = slice(r * N_KEYS, (r + 1) * N_KEYS)
        act = jnp.dot(u_ref[sl, :], xnt, preferred_element_type=F32)
        tiles = (N_KEYS // BF16_ROWS, BF16_ROWS, act.shape[1])
        gate = jnp.zeros(tiles, BF16)
        for h in range(PEER_HEADS):
            cnt_t = jnp.broadcast_to(cnt_ref[h, r:r + 1, :], tiles[1:]).astype(BF16)
            e1_t = jnp.broadcast_to(e1_ref[h, r:r + 1, :], tiles[1:]).astype(BF16)
            picked = rk2_ref[h].reshape(tiles) < cnt_t[None]
            gate = gate + jnp.where(picked, e2_ref[h].reshape(tiles) * e1_t[None], 0.0)
        inner = act * (GELU_C0 + (GELU_C0 * GELU_C1) * (act * act))
        w_parts.append((act * (1.0 + jnp.tanh(inner))).astype(BF16) * gate.reshape(act.shape))
    w = jnp.concatenate(w_parts, axis=0)
    acc_sc[...] += jnp.dot(vt_ref[...], w, preferred_element_type=F32)

    @pl.when(e == pl.num_programs(1) - 1)
    def _():
        out_ref[...] = _rms(x1_ref[...] + acc_sc[...].T, fg_ref[...])


def _peer_experts(xnt, u_bf16, vt_bf16, cnt, e1, rk2, e2, x1, final_g):
    m = xnt.shape[1]
    n_exp = u_bf16.shape[0]
    tm, te = TM_PEER, TE_PEER
    rows = te // N_KEYS
    per_row = pl.BlockSpec((PEER_HEADS, rows, tm), lambda t, e: (0, e, t))
    per_key = pl.BlockSpec((PEER_HEADS, N_KEYS, tm), lambda t, e: (0, 0, t))
    return pl.pallas_call(
        _peer_kernel,
        grid=(m // tm, n_exp // te),
        in_specs=[
            pl.BlockSpec((D_MODEL, tm), lambda t, e: (0, t)),
            pl.BlockSpec((te, D_MODEL), lambda t, e: (e, 0)),
            pl.BlockSpec((D_MODEL, te), lambda t, e: (0, e)),
            per_row, per_row, per_key, per_key,
            pl.BlockSpec((tm, D_MODEL), lambda t, e: (t, 0)),
            pl.BlockSpec((1, D_MODEL), lambda t, e: (0, 0)),
        ],
        out_specs=pl.BlockSpec((tm, D_MODEL), lambda t, e: (t, 0)),
        out_shape=jax.ShapeDtypeStruct((m, D_MODEL), F32),
        scratch_shapes=[pltpu.VMEM((D_MODEL, tm), F32)],
        compiler_params=pltpu.CompilerParams(
            dimension_semantics=("parallel", "arbitrary"), vmem_limit_bytes=VMEM_LIMIT),
        name="peer_experts",
    )(xnt, u_bf16, vt_bf16, cnt, e1, rk2, e2, x1, final_g.reshape(1, D_MODEL))


def kernel(x, mix_norm_g, w_in, conv_w, conv_b, attn_out_g, conv_out_g, w_out, ffn_norm_g,
           peer_wq, peer_keys, peer_u, peer_v, final_norm_g):
    b, s, d = x.shape
    assert w_in.shape[0] == 1, "single-layer block"
    w_in_b = w_in[0].astype(BF16)
    w_out_b = w_out[0].astype(BF16)
    wq_b = peer_wq[0].astype(BF16)
    keys_b = peer_keys[0].astype(BF16)
    u_b = peer_u[0].astype(BF16)
    vt_b = peer_v[0].T.astype(BF16)
    m = b * s
    x2 = x.reshape(m, d)
    qkv, cv = _inproj(x2, mix_norm_g[0], w_in_b)
    oa = _attention(qkv.reshape(b, s, 3 * SB_WIDTH)).reshape(m, SB_WIDTH)
    x1, xnt, cnt, e1, rk2, e2 = _mix_route(
        x2, oa, cv, conv_w[0], conv_b[0], attn_out_g[0], conv_out_g[0],
        w_out_b, ffn_norm_g[0], wq_b, keys_b, s)
    out = _peer_experts(xnt, u_b, vt_b, cnt, e1, rk2, e2, x1, final_norm_g)
    return out.reshape(b, s, d)
```

```python
import functools
import math

import jax
import jax.numpy as jnp
from jax import lax
from jax.experimental import pallas as pl
from jax.experimental.pallas import tpu as pltpu

F32 = jnp.float32
BF16 = jnp.bfloat16

D_MODEL = 1024
SB_HEADS = 8
HEAD_DIM = 64
SB_WIDTH = SB_HEADS * HEAD_DIM
CONV_WIDTH = D_MODEL - SB_WIDTH
CONV_K = 3
N_KEYS = 128
PEER_HEADS = 8
PEER_DK = 256
D_HALF = PEER_DK // 2
PEER_TOPK = 16
EPS = 1e-6
GELU_C0 = math.sqrt(2.0 / math.pi)
GELU_C1 = 0.044715

SUBLANES = 8
LANES = 128
BF16_ROWS = 16
V7X_VMEM_BYTES = 64 * 1024 * 1024

TM_PROJ = 512
TQ = 512
HEADS_PER_STEP = 4
TK = 256
TM_MIX = 512
TM_PEER = 512
TE_PEER = 2048
VMEM_LIMIT = V7X_VMEM_BYTES * 7 // 8


def _rms(x, g):
    return x * lax.rsqrt(jnp.mean(x * x, axis=-1, keepdims=True) + EPS) * g


def _inproj_kernel(x_ref, g_ref, w_ref, qkv_ref, cv_ref):
    h = _rms(x_ref[...], g_ref[...])
    p = jnp.dot(h.astype(BF16), w_ref[...], preferred_element_type=F32)
    qkv_ref[...] = p[:, : 3 * SB_WIDTH].astype(BF16)
    cv_ref[...] = p[:, 3 * SB_WIDTH:]


def _inproj(x2, g, w_bf16):
    m = x2.shape[0]
    n = w_bf16.shape[1]
    return pl.pallas_call(
        _inproj_kernel,
        grid=(m // TM_PROJ,),
        in_specs=[
            pl.BlockSpec((TM_PROJ, D_MODEL), lambda i: (i, 0)),
            pl.BlockSpec((1, D_MODEL), lambda i: (0, 0)),
            pl.BlockSpec((D_MODEL, n), lambda i: (0, 0)),
        ],
        out_specs=[
            pl.BlockSpec((TM_PROJ, 3 * SB_WIDTH), lambda i: (i, 0)),
            pl.BlockSpec((TM_PROJ, 3 * CONV_WIDTH), lambda i: (i, 0)),
        ],
        out_shape=[
            jax.ShapeDtypeStruct((m, 3 * SB_WIDTH), BF16),
            jax.ShapeDtypeStruct((m, 3 * CONV_WIDTH), F32),
        ],
        compiler_params=pltpu.CompilerParams(
            dimension_semantics=("parallel",), vmem_limit_bytes=VMEM_LIMIT),
        name="inproj",
    )(x2, g.reshape(1, D_MODEL), w_bf16)


def _softplus(z):
    return jnp.maximum(z, 0.0) + jnp.log(1.0 + jnp.exp(-jnp.abs(z)))


def _attn_kernel(q_ref, k_ref, v_ref, o_ref):
    i = pl.program_id(2)
    scale = 1.0 / math.sqrt(HEAD_DIM)
    kr = lax.broadcasted_iota(jnp.int32, (TK, TK), 0)
    kc = lax.broadcasted_iota(jnp.int32, (TK, TK), 1)
    tri = (kr >= kc).astype(BF16)
    rows = lax.broadcasted_iota(jnp.int32, (TQ, TK), 0)
    cols = lax.broadcasted_iota(jnp.int32, (TQ, TK), 1)
    kb = TQ // TK
    heads = [slice(hh * HEAD_DIM, (hh + 1) * HEAD_DIM) for hh in range(HEADS_PER_STEP)]
    qs = [q_ref[:, sl] * scale for sl in heads]
    nt = (((1,), (1,)), ((), ()))

    def block(s0, carry, mask):
        zs = [lax.dot_general(q, k_ref[pl.ds(s0, TK), sl], nt, preferred_element_type=F32)
              for q, sl in zip(qs, heads)]
        ls = [-_softplus(z) for z in zs]
        if mask is not None:
            ls = [jnp.where(mask, l, 0.0) for l in ls]
        his = [l.astype(BF16) for l in ls]
        los = [(l - hi.astype(F32)).astype(BF16) for l, hi in zip(ls, his)]
        cs = [jnp.dot(hi, tri, preferred_element_type=F32)
              + jnp.dot(lo, tri, preferred_element_type=F32) + r
              for hi, lo, (_, r) in zip(his, los, carry)]
        ps = [jnp.exp(z + c) for z, c in zip(zs, cs)]
        if mask is not None:
            ps = [jnp.where(mask, p, 0.0) for p in ps]
        os = [o + jnp.dot(p.astype(BF16), v_ref[pl.ds(s0, TK), sl], preferred_element_type=F32)
              for p, sl, (o, _) in zip(ps, heads, carry)]
        return tuple((o, c[:, 0:1]) for o, c in zip(os, cs))

    carry = tuple((jnp.zeros((TQ, HEAD_DIM), F32), jnp.zeros((TQ, 1), F32)) for _ in heads)
    for d in reversed(range(kb)):
        carry = block(pl.multiple_of(i * TQ + d * TK, TK), carry, rows > cols + d * TK)

    def body(jj, carry):
        return block(pl.multiple_of((i * kb - 1 - jj) * TK, TK), carry, None)

    carry = lax.fori_loop(0, i * kb, body, carry)
    o_ref[...] = jnp.concatenate([o for o, _ in carry], axis=1)


def _attention(qkv3):
    b, s, _ = qkv3.shape
    hp = SB_HEADS // HEADS_PER_STEP
    hw = HEADS_PER_STEP * HEAD_DIM
    return pl.pallas_call(
        _attn_kernel,
        grid=(b, hp, s // TQ),
        in_specs=[
            pl.BlockSpec((None, TQ, hw), lambda bi, h, i: (bi, i, h)),
            pl.BlockSpec((None, s, hw), lambda bi, h, i: (bi, 0, hp + h)),
            pl.BlockSpec((None, s, hw), lambda bi, h, i: (bi, 0, 2 * hp + h)),
        ],
        out_specs=pl.BlockSpec((None, TQ, hw), lambda bi, h, i: (bi, i, h)),
        out_shape=jax.ShapeDtypeStruct((b, s, SB_WIDTH), F32),
        compiler_params=pltpu.CompilerParams(
            dimension_semantics=("parallel", "parallel", "arbitrary"),
            vmem_limit_bytes=VMEM_LIMIT),
        name="sb_attention",
    )(qkv3, qkv3, qkv3)


def _extract_topk(s, k, want_rank, exact):
    n = s.shape[0]
    s0 = s
    iota = lax.broadcasted_iota(jnp.int32, s.shape, 0).astype(F32) if exact else None
    rank = jnp.full(s.shape, float(k), F32) if want_rank else None
    vals, idxs = [], []
    for r in range(k):
        m = jnp.max(s, axis=0, keepdims=True)
        if exact:
            im = jnp.min(jnp.where(s == m, iota, float(n)), axis=0, keepdims=True)
            hit = iota == im
            idxs.append(im)
        else:
            hit = s == m
        vals.append(m)
        if want_rank:
            rank = jnp.where(hit, float(r), rank)
        s = jnp.where(hit, -jnp.inf, s)
    taken = jnp.sum(jnp.where(s == s0, 0.0, 1.0), axis=0, keepdims=True)
    rows = jnp.concatenate(idxs, axis=0) if exact else None
    return jnp.concatenate(vals, axis=0), rows, rank, s, taken


def _mix_route_kernel(tiles_per_seq, x_ref, oa_ref, gb_ref, gc_ref, hc_ref, gcp_ref, hcp_ref,
                      cw_ref, cb_ref, ag_ref, cg_ref, wo_ref, fg_ref, wq_ref, keys_ref,
                      x1_ref, xnt_ref, cnt_ref, e1_ref, rk2_ref, e2_ref, q_sc):
    i = pl.program_id(0)
    tm = x_ref.shape[0]
    u = gc_ref[...] * hc_ref[...]
    first = (i % tiles_per_seq) == 0
    u_prev = jnp.where(first, 0.0, gcp_ref[...] * hcp_ref[...])
    u_ext = jnp.concatenate([u_prev, u], axis=0)
    cw = cw_ref[...]
    y = (cw[0:1, :] * u_ext[SUBLANES - 2:SUBLANES - 2 + tm, :]
         + cw[1:2, :] * u_ext[SUBLANES - 1:SUBLANES - 1 + tm, :]
         + cw[2:3, :] * u) + cb_ref[...]
    o_conv = gb_ref[...] * y
    mixed = jnp.concatenate([_rms(oa_ref[...], ag_ref[...]), _rms(o_conv, cg_ref[...])], axis=1)
    x1 = x_ref[...] + jnp.dot(mixed.astype(BF16), wo_ref[...], preferred_element_type=F32)
    x1_ref[...] = x1
    xn = _rms(x1, fg_ref[...]).astype(BF16)
    xnt_ref[...] = xn.T
    q_sc[...] = jnp.dot(xn, wq_ref[...], preferred_element_type=F32)

    def head(h, carry):
        c0 = pl.multiple_of(h * PEER_DK, PEER_DK)
        q1 = q_sc[:, pl.ds(c0, D_HALF)].astype(BF16)
        q2 = q_sc[:, pl.ds(c0 + D_HALF, D_HALF)].astype(BF16)
        nt = (((1,), (1,)), ((), ()))
        s1 = lax.dot_general(keys_ref[0, h], q1, nt, preferred_element_type=F32)
        s2 = lax.dot_general(keys_ref[1, h], q2, nt, preferred_element_type=F32)

        def route(exact):
            v1, i1, _, _, took1 = _extract_topk(s1, PEER_TOPK, False, exact)
            v2, _, rk2, _, took2 = _extract_topk(s2, PEER_TOPK, True, exact)
            widths = [PEER_TOPK // (a + 1) for a in range(PEER_TOPK)]
            pad = -sum(widths) % SUBLANES
            cand = jnp.concatenate(
                [v1[a:a + 1, :] + v2[0:widths[a], :] for a in range(PEER_TOPK)]
                + [jnp.full((pad, tm), -jnp.inf, F32)], axis=0)
            best, _, _, left, took = _extract_topk(cand, PEER_TOPK, False, exact)
            picked = jnp.where(left == cand, 0.0, 1.0)
            z = jnp.sum(jnp.exp(best - best[0:1, :]), axis=0, keepdims=True)
            cnt = jnp.zeros_like(s1)
            key_row = lax.broadcasted_iota(jnp.int32, s1.shape, 0).astype(F32) if exact else None
            off = 0
            for a in range(PEER_TOPK):
                n_a = jnp.sum(picked[off:off + widths[a], :], axis=0, keepdims=True)
                at_a = key_row == i1[a:a + 1, :] if exact else s1 == v1[a:a + 1, :]
                cnt = jnp.where(at_a, n_a, cnt)
                off += widths[a]
            cnt_ref[h] = cnt
            e1_ref[h] = jnp.exp(s1 - v1[0:1, :])
            rk2_ref[h] = rk2.astype(BF16)
            e2_ref[h] = (jnp.exp(s2 - v2[0:1, :]) * (0.5 / z)).astype(BF16)
            k = float(PEER_TOPK)
            return jnp.where((took1 == k) & (took2 == k) & (took == k), 0.0, 1.0)

        tied = route(exact=False)

        @pl.when(jnp.max(tied) > 0.0)
        def _():
            route(exact=True)

        return carry

    lax.fori_loop(0, PEER_HEADS, head, 0)


def _mix_route(x2, oa, cv, conv_w, conv_b, attn_g, conv_g, wo_bf16, ffn_g, wq_bf16, keys_bf16, seq):
    m = x2.shape[0]
    tm = TM_MIX
    row = lambda i: (i, 0)
    const2 = lambda i: (0, 0)
    halo = tm // SUBLANES
    return pl.pallas_call(
        functools.partial(_mix_route_kernel, seq // tm),
        grid=(m // tm,),
        in_specs=[
            pl.BlockSpec((tm, D_MODEL), row),
            pl.BlockSpec((tm, SB_WIDTH), row),
            pl.BlockSpec((tm, CONV_WIDTH), lambda i: (i, 0)),
            pl.BlockSpec((tm, CONV_WIDTH), lambda i: (i, 1)),
            pl.BlockSpec((tm, CONV_WIDTH), lambda i: (i, 2)),
            pl.BlockSpec((SUBLANES, CONV_WIDTH), lambda i: (jnp.maximum(i * halo - 1, 0), 1)),
            pl.BlockSpec((SUBLANES, CONV_WIDTH), lambda i: (jnp.maximum(i * halo - 1, 0), 2)),
            pl.BlockSpec((CONV_K, CONV_WIDTH), const2),
            pl.BlockSpec((1, CONV_WIDTH), const2),
            pl.BlockSpec((1, SB_WIDTH), const2),
            pl.BlockSpec((1, CONV_WIDTH), const2),
            pl.BlockSpec((D_MODEL, D_MODEL), const2),
            pl.BlockSpec((1, D_MODEL), const2),
            pl.BlockSpec((D_MODEL, PEER_HEADS * PEER_DK), const2),
            pl.BlockSpec((2, PEER_HEADS, N_KEYS, D_HALF), lambda i: (0, 0, 0, 0)),
        ],
        out_specs=[
            pl.BlockSpec((tm, D_MODEL), row),
            pl.BlockSpec((D_MODEL, tm), lambda i: (0, i)),
        ] + [pl.BlockSpec((PEER_HEADS, N_KEYS, tm), lambda i: (0, 0, i))] * 4,
        out_shape=[
            jax.ShapeDtypeStruct((m, D_MODEL), F32),
            jax.ShapeDtypeStruct((D_MODEL, m), BF16),
        ] + [jax.ShapeDtypeStruct((PEER_HEADS, N_KEYS, m), dt) for dt in (F32, F32, BF16, BF16)],
        scratch_shapes=[pltpu.VMEM((tm, PEER_HEADS * PEER_DK), F32)],
        compiler_params=pltpu.CompilerParams(
            dimension_semantics=("parallel",), vmem_limit_bytes=VMEM_LIMIT),
        name="mix_route",
    )(x2, oa, cv, cv, cv, cv, cv, conv_w, conv_b.reshape(1, -1), attn_g.reshape(1, -1),
      conv_g.reshape(1, -1), wo_bf16, ffn_g.reshape(1, -1), wq_bf16, keys_bf16)


def _peer_kernel(xnt_ref, u_ref, vt_ref, cnt_ref, e1_ref, rk2_ref, e2_ref, x1_ref, fg_ref,
                 out_ref, acc_sc):
    e = pl.program_id(1)

    @pl.when(e == 0)
    def _():
        acc_sc[...] = jnp.zeros_like(acc_sc)

    xnt = xnt_ref[...]
    rows = TE_PEER // N_KEYS
    w_parts = []
    for r in range(rows):
        sl = slice(r * N_KEYS, (r + 1) * N_KEYS)
        act = jnp.dot(u_ref[sl, :], xnt, preferred_element_type=F32)
        tiles = (N_KEYS // BF16_ROWS, BF16_ROWS, act.shape[1])
        gate = jnp.zeros(tiles, BF16)
        for h in range(PEER_HEADS):
            cnt_t = jnp.broadcast_to(cnt_ref[h, r:r + 1, :], tiles[1:]).astype(BF16)
            e1_t = jnp.broadcast_to(e1_ref[h, r:r + 1, :], tiles[1:]).astype(BF16)
            picked = rk2_ref[h].reshape(tiles) < cnt_t[None]
            gate = gate + jnp.where(picked, e2_ref[h].reshape(tiles) * e1_t[None], 0.0)
        inner = act * (GELU_C0 + (GELU_C0 * GELU_C1) * (act * act))
        w_parts.append((act * (1.0 + jnp.tanh(inner))).astype(BF16) * gate.reshape(act.shape))
    w = jnp.concatenate(w_parts, axis=0)
    acc_sc[...] += jnp.dot(vt_ref[...], w, preferred_element_type=F32)

    @pl.when(e == pl.num_programs(1) - 1)
    def _():
        out_ref[...] = _rms(x1_ref[...] + acc_sc[...].T, fg_ref[...])


def _peer_experts(xnt, u_bf16, vt_bf16, cnt, e1, rk2, e2, x1, final_g):
    m = xnt.shape[1]
    n_exp = u_bf16.shape[0]
    tm, te = TM_PEER, TE_PEER
    rows = te // N_KEYS
    per_row = pl.BlockSpec((PEER_HEADS, rows, tm), lambda t, e: (0, e, t))
    per_key = pl.BlockSpec((PEER_HEADS, N_KEYS, tm), lambda t, e: (0, 0, t))
    return pl.pallas_call(
        _peer_kernel,
        grid=(m // tm, n_exp // te),
        in_specs=[
            pl.BlockSpec((D_MODEL, tm), lambda t, e: (0, t)),
            pl.BlockSpec((te, D_MODEL), lambda t, e: (e, 0)),
            pl.BlockSpec((D_MODEL, te), lambda t, e: (0, e)),
            per_row, per_row, per_key, per_key,
            pl.BlockSpec((tm, D_MODEL), lambda t, e: (t, 0)),
            pl.BlockSpec((1, D_MODEL), lambda t, e: (0, 0)),
        ],
        out_specs=pl.BlockSpec((tm, D_MODEL), lambda t, e: (t, 0)),
        out_shape=jax.ShapeDtypeStruct((m, D_MODEL), F32),
        scratch_shapes=[pltpu.VMEM((D_MODEL, tm), F32)],
        compiler_params=pltpu.CompilerParams(
            dimension_semantics=("parallel", "arbitrary"), vmem_limit_bytes=VMEM_LIMIT),
        name="peer_experts",
    )(xnt, u_bf16, vt_bf16, cnt, e1, rk2, e2, x1, final_g.reshape(1, D_MODEL))


def kernel(x, mix_norm_g, w_in, conv_w, conv_b, attn_out_g, conv_out_g, w_out, ffn_norm_g,
           peer_wq, peer_keys, peer_u, peer_v, final_norm_g):
    b, s, d = x.shape
    assert w_in.shape[0] == 1, "single-layer block"
    w_in_b = w_in[0].astype(BF16)
    w_out_b = w_out[0].astype(BF16)
    wq_b = peer_wq[0].astype(BF16)
    keys_b = peer_keys[0].astype(BF16)
    u_b = peer_u[0].astype(BF16)
    vt_b = peer_v[0].T.astype(BF16)
    m = b * s
    x2 = x.reshape(m, d)
    qkv, cv = _inproj(x2, mix_norm_g[0], w_in_b)
    oa = _attention(qkv.reshape(b, s, 3 * SB_WIDTH)).reshape(m, SB_WIDTH)
    x1, xnt, cnt, e1, rk2, e2 = _mix_route(
        x2, oa, cv, conv_w[0], conv_b[0], attn_out_g[0], conv_out_g[0],
        w_out_b, ffn_norm_g[0], wq_b, keys_b, s)
    out = _peer_experts(xnt, u_b, vt_b, cnt, e1, rk2, e2, x1, final_norm_g)
    return out.reshape(b, s, d)
```

```python
import functools
import math

import jax
import jax.numpy as jnp
from jax import lax
from jax.experimental import pallas as pl
from jax.experimental.pallas import tpu as pltpu

F32 = jnp.float32
BF16 = jnp.bfloat16

D_MODEL = 1024
SB_HEADS = 8
HEAD_DIM = 64
SB_WIDTH = SB_HEADS * HEAD_DIM
CONV_WIDTH = D_MODEL - SB_WIDTH
CONV_K = 3
N_KEYS = 128
PEER_HEADS = 8
PEER_DK = 256
D_HALF = PEER_DK // 2
PEER_TOPK = 16
EPS = 1e-6
GELU_C0 = math.sqrt(2.0 / math.pi)
GELU_C1 = 0.044715

SUBLANES = 8
LANES = 128
BF16_ROWS = 16
V7X_VMEM_BYTES = 64 * 1024 * 1024

TM_PROJ = 512
TQ = 512
HEADS_PER_STEP = 4
TK = 256
TM_MIX = 512
TM_PEER = 512
TE_PEER = 2048
VMEM_LIMIT = V7X_VMEM_BYTES * 7 // 8


def _rms(x, g):
    return x * lax.rsqrt(jnp.mean(x * x, axis=-1, keepdims=True) + EPS) * g


def _inproj_kernel(x_ref, g_ref, w_ref, qkv_ref, cv_ref):
    h = _rms(x_ref[...], g_ref[...])
    p = jnp.dot(h.astype(BF16), w_ref[...], preferred_element_type=F32)
    qkv_ref[...] = p[:, : 3 * SB_WIDTH].astype(BF16)
    cv_ref[...] = p[:, 3 * SB_WIDTH:]


def _inproj(x2, g, w_bf16):
    m = x2.shape[0]
    n = w_bf16.shape[1]
    return pl.pallas_call(
        _inproj_kernel,
        grid=(m // TM_PROJ,),
        in_specs=[
            pl.BlockSpec((TM_PROJ, D_MODEL), lambda i: (i, 0)),
            pl.BlockSpec((1, D_MODEL), lambda i: (0, 0)),
            pl.BlockSpec((D_MODEL, n), lambda i: (0, 0)),
        ],
        out_specs=[
            pl.BlockSpec((TM_PROJ, 3 * SB_WIDTH), lambda i: (i, 0)),
            pl.BlockSpec((TM_PROJ, 3 * CONV_WIDTH), lambda i: (i, 0)),
        ],
        out_shape=[
            jax.ShapeDtypeStruct((m, 3 * SB_WIDTH), BF16),
            jax.ShapeDtypeStruct((m, 3 * CONV_WIDTH), F32),
        ],
        compiler_params=pltpu.CompilerParams(
            dimension_semantics=("parallel",), vmem_limit_bytes=VMEM_LIMIT),
        name="inproj",
    )(x2, g.reshape(1, D_MODEL), w_bf16)


def _softplus(z):
    return jnp.maximum(z, 0.0) + jnp.log(1.0 + jnp.exp(-jnp.abs(z)))


def _attn_kernel(q_ref, k_ref, v_ref, o_ref):
    i = pl.program_id(2)
    scale = 1.0 / math.sqrt(HEAD_DIM)
    kr = lax.broadcasted_iota(jnp.int32, (TK, TK), 0)
    kc = lax.broadcasted_iota(jnp.int32, (TK, TK), 1)
    tri = (kr >= kc).astype(BF16)
    rows = lax.broadcasted_iota(jnp.int32, (TQ, TK), 0)
    cols = lax.broadcasted_iota(jnp.int32, (TQ, TK), 1)
    kb = TQ // TK
    heads = [slice(hh * HEAD_DIM, (hh + 1) * HEAD_DIM) for hh in range(HEADS_PER_STEP)]
    qs = [q_ref[:, sl] * scale for sl in heads]
    nt = (((1,), (1,)), ((), ()))

    def block(s0, carry, mask):
        zs = [lax.dot_general(q, k_ref[pl.ds(s0, TK), sl], nt, preferred_element_type=F32)
              for q, sl in zip(qs, heads)]
        ls = [-_softplus(z) for z in zs]
        if mask is not None:
            ls = [jnp.where(mask, l, 0.0) for l in ls]
        his = [l.astype(BF16) for l in ls]
        los = [(l - hi.astype(F32)).astype(BF16) for l, hi in zip(ls, his)]
        cs = [jnp.dot(hi, tri, preferred_element_type=F32)
              + jnp.dot(lo, tri, preferred_element_type=F32) + r
              for hi, lo, (_, r) in zip(his, los, carry)]
        ps = [jnp.exp(z + c) for z, c in zip(zs, cs)]
        if mask is not None:
            ps = [jnp.where(mask, p, 0.0) for p in ps]
        os = [o + jnp.dot(p.astype(BF16), v_ref[pl.ds(s0, TK), sl], preferred_element_type=F32)
              for p, sl, (o, _) in zip(ps, heads, carry)]
        return tuple((o, c[:, 0:1]) for o, c in zip(os, cs))

    carry = tuple((jnp.zeros((TQ, HEAD_DIM), F32), jnp.zeros((TQ, 1), F32)) for _ in heads)
    for d in reversed(range(kb)):
        carry = block(pl.multiple_of(i * TQ + d * TK, TK), carry, rows > cols + d * TK)

    def body(jj, carry):
        return block(pl.multiple_of((i * kb - 1 - jj) * TK, TK), carry, None)

    carry = lax.fori_loop(0, i * kb, body, carry)
    o_ref[...] = jnp.concatenate([o for o, _ in carry], axis=1)


def _attention(qkv3):
    b, s, _ = qkv3.shape
    hp = SB_HEADS // HEADS_PER_STEP
    hw = HEADS_PER_STEP * HEAD_DIM
    return pl.pallas_call(
        _attn_kernel,
        grid=(b, hp, s // TQ),
        in_specs=[
            pl.BlockSpec((None, TQ, hw), lambda bi, h, i: (bi, i, h)),
            pl.BlockSpec((None, s, hw), lambda bi, h, i: (bi, 0, hp + h)),
            pl.BlockSpec((None, s, hw), lambda bi, h, i: (bi, 0, 2 * hp + h)),
        ],
        out_specs=pl.BlockSpec((None, TQ, hw), lambda bi, h, i: (bi, i, h)),
        out_shape=jax.ShapeDtypeStruct((b, s, SB_WIDTH), F32),
        compiler_params=pltpu.CompilerParams(
            dimension_semantics=("parallel", "parallel", "arbitrary"),
            vmem_limit_bytes=VMEM_LIMIT),
        name="sb_attention",
    )(qkv3, qkv3, qkv3)


def _extract_topk(s, k, want_rank, exact):
    n = s.shape[0]
    s0 = s
    iota = lax.broadcasted_iota(jnp.int32, s.shape, 0).astype(F32) if exact else None
    rank = jnp.full(s.shape, float(k), F32) if want_rank else None
    vals, idxs = [], []
    for r in range(k):
        m = jnp.max(s, axis=0, keepdims=True)
        if exact:
            im = jnp.min(jnp.where(s == m, iota, float(n)), axis=0, keepdims=True)
            hit = iota == im
            idxs.append(im)
        else:
            hit = s == m
        vals.append(m)
        if want_rank:
            rank = jnp.where(hit, float(r), rank)
        s = jnp.where(hit, -jnp.inf, s)
    taken = jnp.sum(jnp.where(s == s0, 0.0, 1.0), axis=0, keepdims=True)
    rows = jnp.concatenate(idxs, axis=0) if exact else None
    return jnp.concatenate(vals, axis=0), rows, rank, s, taken


def _mix_route_kernel(tiles_per_seq, x_ref, oa_ref, gb_ref, gc_ref, hc_ref, gcp_ref, hcp_ref,
                      cw_ref, cb_ref, ag_ref, cg_ref, wo_ref, fg_ref, wq_ref, keys_ref,
                      x1_ref, xnt_ref, cnt_ref, e1_ref, rk2_ref, e2_ref, q_sc):
    i = pl.program_id(0)
    tm = x_ref.shape[0]
    u = gc_ref[...] * hc_ref[...]
    first = (i % tiles_per_seq) == 0
    u_prev = jnp.where(first, 0.0, gcp_ref[...] * hcp_ref[...])
    u_ext = jnp.concatenate([u_prev, u], axis=0)
    cw = cw_ref[...]
    y = (cw[0:1, :] * u_ext[SUBLANES - 2:SUBLANES - 2 + tm, :]
         + cw[1:2, :] * u_ext[SUBLANES - 1:SUBLANES - 1 + tm, :]
         + cw[2:3, :] * u) + cb_ref[...]
    o_conv = gb_ref[...] * y
    mixed = jnp.concatenate([_rms(oa_ref[...], ag_ref[...]), _rms(o_conv, cg_ref[...])], axis=1)
    x1 = x_ref[...] + jnp.dot(mixed.astype(BF16), wo_ref[...], preferred_element_type=F32)
    x1_ref[...] = x1
    xn = _rms(x1, fg_ref[...]).astype(BF16)
    xnt_ref[...] = xn.T
    q_sc[...] = jnp.dot(xn, wq_ref[...], preferred_element_type=F32)

    def head(h, carry):
        c0 = pl.multiple_of(h * PEER_DK, PEER_DK)
        q1 = q_sc[:, pl.ds(c0, D_HALF)].astype(BF16)
        q2 = q_sc[:, pl.ds(c0 + D_HALF, D_HALF)].astype(BF16)
        nt = (((1,), (1,)), ((), ()))
        s1 = lax.dot_general(keys_ref[0, h], q1, nt, preferred_element_type=F32)
        s2 = lax.dot_general(keys_ref[1, h], q2, nt, preferred_element_type=F32)

        def route(exact):
            v1, i1, _, _, took1 = _extract_topk(s1, PEER_TOPK, False, exact)
            v2, _, rk2, _, took2 = _extract_topk(s2, PEER_TOPK, True, exact)
            widths = [PEER_TOPK // (a + 1) for a in range(PEER_TOPK)]
            pad = -sum(widths) % SUBLANES
            cand = jnp.concatenate(
                [v1[a:a + 1, :] + v2[0:widths[a], :] for a in range(PEER_TOPK)]
                + [jnp.full((pad, tm), -jnp.inf, F32)], axis=0)
            best, _, _, left, took = _extract_topk(cand, PEER_TOPK, False, exact)
            picked = jnp.where(left == cand, 0.0, 1.0)
            z = jnp.sum(jnp.exp(best - best[0:1, :]), axis=0, keepdims=True)
            cnt = jnp.zeros_like(s1)
            key_row = lax.broadcasted_iota(jnp.int32, s1.shape, 0).astype(F32) if exact else None
            off = 0
            for a in range(PEER_TOPK):
                n_a = jnp.sum(picked[off:off + widths[a], :], axis=0, keepdims=True)
                at_a = key_row == i1[a:a + 1, :] if exact else s1 == v1[a:a + 1, :]
                cnt = jnp.where(at_a, n_a, cnt)
                off += widths[a]
            cnt_ref[h] = cnt
            e1_ref[h] = jnp.exp(s1 - v1[0:1, :])
            rk2_ref[h] = rk2.astype(BF16)
            e2_ref[h] = (jnp.exp(s2 - v2[0:1, :]) * (0.5 / z)).astype(BF16)
            k = float(PEER_TOPK)
            return jnp.where((took1 == k) & (took2 == k) & (took == k), 0.0, 1.0)

        tied = route(exact=False)

        @pl.when(jnp.max(tied) > 0.0)
        def _():
            route(exact=True)

        return carry

    lax.fori_loop(0, PEER_HEADS, head, 0)


def _mix_route(x2, oa, cv, conv_w, conv_b, attn_g, conv_g, wo_bf16, ffn_g, wq_bf16, keys_bf16, seq):
    m = x2.shape[0]
    tm = TM_MIX
    row = lambda i: (i, 0)
    const2 = lambda i: (0, 0)
    halo = tm // SUBLANES
    return pl.pallas_call(
        functools.partial(_mix_route_kernel, seq // tm),
        grid=(m // tm,),
        in_specs=[
            pl.BlockSpec((tm, D_MODEL), row),
            pl.BlockSpec((tm, SB_WIDTH), row),
            pl.BlockSpec((tm, CONV_WIDTH), lambda i: (i, 0)),
            pl.BlockSpec((tm, CONV_WIDTH), lambda i: (i, 1)),
            pl.BlockSpec((tm, CONV_WIDTH), lambda i: (i, 2)),
            pl.BlockSpec((SUBLANES, CONV_WIDTH), lambda i: (jnp.maximum(i * halo - 1, 0), 1)),
            pl.BlockSpec((SUBLANES, CONV_WIDTH), lambda i: (jnp.maximum(i * halo - 1, 0), 2)),
            pl.BlockSpec((CONV_K, CONV_WIDTH), const2),
            pl.BlockSpec((1, CONV_WIDTH), const2),
            pl.BlockSpec((1, SB_WIDTH), const2),
            pl.BlockSpec((1, CONV_WIDTH), const2),
            pl.BlockSpec((D_MODEL, D_MODEL), const2),
            pl.BlockSpec((1, D_MODEL), const2),
            pl.BlockSpec((D_MODEL, PEER_HEADS * PEER_DK), const2),
            pl.BlockSpec((2, PEER_HEADS, N_KEYS, D_HALF), lambda i: (0, 0, 0, 0)),
        ],
        out_specs=[
            pl.BlockSpec((tm, D_MODEL), row),
            pl.BlockSpec((D_MODEL, tm), lambda i: (0, i)),
        ] + [pl.BlockSpec((PEER_HEADS, N_KEYS, tm), lambda i: (0, 0, i))] * 4,
        out_shape=[
            jax.ShapeDtypeStruct((m, D_MODEL), F32),
            jax.ShapeDtypeStruct((D_MODEL, m), BF16),
        ] + [jax.ShapeDtypeStruct((PEER_HEADS, N_KEYS, m), dt) for dt in (F32, F32, BF16, BF16)],
        scratch_shapes=[pltpu.VMEM((tm, PEER_HEADS * PEER_DK), F32)],
        compiler_params=pltpu.CompilerParams(
            dimension_semantics=("parallel",), vmem_limit_bytes=VMEM_LIMIT),
        name="mix_route",
    )(x2, oa, cv, cv, cv, cv, cv, conv_w, conv_b.reshape(1, -1), attn_g.reshape(1, -1),
      conv_g.reshape(1, -1), wo_bf16, ffn_g.reshape(1, -1), wq_bf16, keys_bf16)


def _peer_kernel(xnt_ref, u_ref, vt_ref, cnt_ref, e1_ref, rk2_ref, e2_ref, x1_ref, fg_ref,
                 out_ref, acc_sc):
    e = pl.program_id(1)

    @pl.when(e == 0)
    def _():
        acc_sc[...] = jnp.zeros_like(acc_sc)

    xnt = xnt_ref[...]
    rows = TE_PEER // N_KEYS
    w_parts = []
    for r in range(rows):
        sl = slice(r * N_KEYS, (r + 1) * N_KEYS)
        act = jnp.dot(u_ref[sl, :], xnt, preferred_element_type=F32)
        tiles = (N_KEYS // BF16_ROWS, BF16_ROWS, act.shape[1])
        gate = jnp.zeros(tiles, BF16)
        for h in range(PEER_HEADS):
            cnt_t = jnp.broadcast_to(cnt_ref[h, r:r + 1, :], tiles[1:]).astype(BF16)
            e1_t = jnp.broadcast_to(e1_ref[h, r:r + 1, :], tiles[1:]).astype(BF16)
            picked = rk2_ref[h].reshape(tiles) < cnt_t[None]
            gate = gate + jnp.where(picked, e2_ref[h].reshape(tiles) * e1_t[None], 0.0)
        a_b = act.astype(BF16)
        inner = a_b * (GELU_C0 + (GELU_C0 * GELU_C1) * (a_b * a_b))
        w_parts.append(a_b * (1.0 + jnp.tanh(inner)) * gate.reshape(act.shape))
    w = jnp.concatenate(w_parts, axis=0)
    acc_sc[...] += jnp.dot(vt_ref[...], w, preferred_element_type=F32)

    @pl.when(e == pl.num_programs(1) - 1)
    def _():
        out_ref[...] = _rms(x1_ref[...] + acc_sc[...].T, fg_ref[...])


def _peer_experts(xnt, u_bf16, vt_bf16, cnt, e1, rk2, e2, x1, final_g):
    m = xnt.shape[1]
    n_exp = u_bf16.shape[0]
    tm, te = TM_PEER, TE_PEER
    rows = te // N_KEYS
    per_row = pl.BlockSpec((PEER_HEADS, rows, tm), lambda t, e: (0, e, t))
    per_key = pl.BlockSpec((PEER_HEADS, N_KEYS, tm), lambda t, e: (0, 0, t))
    return pl.pallas_call(
        _peer_kernel,
        grid=(m // tm, n_exp // te),
        in_specs=[
            pl.BlockSpec((D_MODEL, tm), lambda t, e: (0, t)),
            pl.BlockSpec((te, D_MODEL), lambda t, e: (e, 0)),
            pl.BlockSpec((D_MODEL, te), lambda t, e: (0, e)),
            per_row, per_row, per_key, per_key,
            pl.BlockSpec((tm, D_MODEL), lambda t, e: (t, 0)),
            pl.BlockSpec((1, D_MODEL), lambda t, e: (0, 0)),
        ],
        out_specs=pl.BlockSpec((tm, D_MODEL), lambda t, e: (t, 0)),
        out_shape=jax.ShapeDtypeStruct((m, D_MODEL), F32),
        scratch_shapes=[pltpu.VMEM((D_MODEL, tm), F32)],
        compiler_params=pltpu.CompilerParams(
            dimension_semantics=("parallel", "arbitrary"), vmem_limit_bytes=VMEM_LIMIT),
        name="peer_experts",
    )(xnt, u_bf16, vt_bf16, cnt, e1, rk2, e2, x1, final_g.reshape(1, D_MODEL))


def kernel(x, mix_norm_g, w_in, conv_w, conv_b, attn_out_g, conv_out_g, w_out, ffn_norm_g,
           peer_wq, peer_keys, peer_u, peer_v, final_norm_g):
    b, s, d = x.shape
    assert w_in.shape[0] == 1, "single-layer block"
    w_in_b = w_in[0].astype(BF16)
    w_out_b = w_out[0].astype(BF16)
    wq_b = peer_wq[0].astype(BF16)
    keys_b = peer_keys[0].astype(BF16)
    u_b = peer_u[0].astype(BF16)
    vt_b = peer_v[0].T.astype(BF16)
    m = b * s
    x2 = x.reshape(m, d)
    qkv, cv = _inproj(x2, mix_norm_g[0], w_in_b)
    oa = _attention(qkv.reshape(b, s, 3 * SB_WIDTH)).reshape(m, SB_WIDTH)
    x1, xnt, cnt, e1, rk2, e2 = _mix_route(
        x2, oa, cv, conv_w[0], conv_b[0], attn_out_g[0], conv_out_g[0],
        w_out_b, ffn_norm_g[0], wq_b, keys_b, s)
    out = _peer_experts(xnt, u_b, vt_b, cnt, e1, rk2, e2, x1, final_norm_g)
    return out.reshape(b, s, d)
```

```python
import functools
import math

import jax
import jax.numpy as jnp
from jax import lax
from jax.experimental import pallas as pl
from jax.experimental.pallas import tpu as pltpu

F32 = jnp.float32
BF16 = jnp.bfloat16

D_MODEL = 1024
SB_HEADS = 8
HEAD_DIM = 64
SB_WIDTH = SB_HEADS * HEAD_DIM
CONV_WIDTH = D_MODEL - SB_WIDTH
CONV_K = 3
N_KEYS = 128
PEER_HEADS = 8
PEER_DK = 256
D_HALF = PEER_DK // 2
PEER_TOPK = 16
EPS = 1e-6
GELU_C0 = math.sqrt(2.0 / math.pi)
GELU_C1 = 0.044715

SUBLANES = 8
LANES = 128
BF16_ROWS = 16
V7X_VMEM_BYTES = 64 * 1024 * 1024

TM_PROJ = 512
TQ = 512
HEADS_PER_STEP = 8
TK = 256
TM_MIX = 512
TM_PEER = 512
TE_PEER = 2048
VMEM_LIMIT = V7X_VMEM_BYTES * 7 // 8


def _rms(x, g):
    return x * lax.rsqrt(jnp.mean(x * x, axis=-1, keepdims=True) + EPS) * g


def _inproj_kernel(x_ref, g_ref, w_ref, qkv_ref, cv_ref):
    h = _rms(x_ref[...], g_ref[...])
    p = jnp.dot(h.astype(BF16), w_ref[...], preferred_element_type=F32)
    qkv_ref[...] = p[:, : 3 * SB_WIDTH].astype(BF16)
    cv_ref[...] = p[:, 3 * SB_WIDTH:]


def _inproj(x2, g, w_bf16):
    m = x2.shape[0]
    n = w_bf16.shape[1]
    return pl.pallas_call(
        _inproj_kernel,
        grid=(m // TM_PROJ,),
        in_specs=[
            pl.BlockSpec((TM_PROJ, D_MODEL), lambda i: (i, 0)),
            pl.BlockSpec((1, D_MODEL), lambda i: (0, 0)),
            pl.BlockSpec((D_MODEL, n), lambda i: (0, 0)),
        ],
        out_specs=[
            pl.BlockSpec((TM_PROJ, 3 * SB_WIDTH), lambda i: (i, 0)),
            pl.BlockSpec((TM_PROJ, 3 * CONV_WIDTH), lambda i: (i, 0)),
        ],
        out_shape=[
            jax.ShapeDtypeStruct((m, 3 * SB_WIDTH), BF16),
            jax.ShapeDtypeStruct((m, 3 * CONV_WIDTH), F32),
        ],
        compiler_params=pltpu.CompilerParams(
            dimension_semantics=("parallel",), vmem_limit_bytes=VMEM_LIMIT),
        name="inproj",
    )(x2, g.reshape(1, D_MODEL), w_bf16)


def _softplus(z):
    return jnp.maximum(z, 0.0) + jnp.log(1.0 + jnp.exp(-jnp.abs(z)))


def _attn_kernel(q_ref, k_ref, v_ref, o_ref):
    i = pl.program_id(2)
    scale = 1.0 / math.sqrt(HEAD_DIM)
    kr = lax.broadcasted_iota(jnp.int32, (TK, TK), 0)
    kc = lax.broadcasted_iota(jnp.int32, (TK, TK), 1)
    tri = (kr >= kc).astype(BF16)
    rows = lax.broadcasted_iota(jnp.int32, (TQ, TK), 0)
    cols = lax.broadcasted_iota(jnp.int32, (TQ, TK), 1)
    kb = TQ // TK
    heads = [slice(hh * HEAD_DIM, (hh + 1) * HEAD_DIM) for hh in range(HEADS_PER_STEP)]
    qs = [q_ref[:, sl] * scale for sl in heads]
    nt = (((1,), (1,)), ((), ()))

    def block(s0, carry, mask):
        zs = [lax.dot_general(q, k_ref[pl.ds(s0, TK), sl], nt, preferred_element_type=F32)
              for q, sl in zip(qs, heads)]
        ls = [-_softplus(z) for z in zs]
        if mask is not None:
            ls = [jnp.where(mask, l, 0.0) for l in ls]
        his = [l.astype(BF16) for l in ls]
        los = [(l - hi.astype(F32)).astype(BF16) for l, hi in zip(ls, his)]
        cs = [jnp.dot(hi, tri, preferred_element_type=F32)
              + jnp.dot(lo, tri, preferred_element_type=F32) + r
              for hi, lo, (_, r) in zip(his, los, carry)]
        ps = [jnp.exp(z + c) for z, c in zip(zs, cs)]
        if mask is not None:
            ps = [jnp.where(mask, p, 0.0) for p in ps]
        os = [o + jnp.dot(p.astype(BF16), v_ref[pl.ds(s0, TK), sl], preferred_element_type=F32)
              for p, sl, (o, _) in zip(ps, heads, carry)]
        return tuple((o, c[:, 0:1]) for o, c in zip(os, cs))

    carry = tuple((jnp.zeros((TQ, HEAD_DIM), F32), jnp.zeros((TQ, 1), F32)) for _ in heads)
    for d in reversed(range(kb)):
        carry = block(pl.multiple_of(i * TQ + d * TK, TK), carry, rows > cols + d * TK)

    def body(jj, carry):
        return block(pl.multiple_of((i * kb - 1 - jj) * TK, TK), carry, None)

    carry = lax.fori_loop(0, i * kb, body, carry)
    o_ref[...] = jnp.concatenate([o for o, _ in carry], axis=1)


def _attention(qkv3):
    b, s, _ = qkv3.shape
    hp = SB_HEADS // HEADS_PER_STEP
    hw = HEADS_PER_STEP * HEAD_DIM
    return pl.pallas_call(
        _attn_kernel,
        grid=(b, hp, s // TQ),
        in_specs=[
            pl.BlockSpec((None, TQ, hw), lambda bi, h, i: (bi, i, h)),
            pl.BlockSpec((None, s, hw), lambda bi, h, i: (bi, 0, hp + h)),
            pl.BlockSpec((None, s, hw), lambda bi, h, i: (bi, 0, 2 * hp + h)),
        ],
        out_specs=pl.BlockSpec((None, TQ, hw), lambda bi, h, i: (bi, i, h)),
        out_shape=jax.ShapeDtypeStruct((b, s, SB_WIDTH), F32),
        compiler_params=pltpu.CompilerParams(
            dimension_semantics=("parallel", "parallel", "arbitrary"),
            vmem_limit_bytes=VMEM_LIMIT),
        name="sb_attention",
    )(qkv3, qkv3, qkv3)


def _extract_topk(s, k, want_rank, exact):
    n = s.shape[0]
    s0 = s
    iota = lax.broadcasted_iota(jnp.int32, s.shape, 0).astype(F32) if exact else None
    rank = jnp.full(s.shape, float(k), F32) if want_rank else None
    vals, idxs = [], []
    for r in range(k):
        m = jnp.max(s, axis=0, keepdims=True)
        if exact:
            im = jnp.min(jnp.where(s == m, iota, float(n)), axis=0, keepdims=True)
            hit = iota == im
            idxs.append(im)
        else:
            hit = s == m
        vals.append(m)
        if want_rank:
            rank = jnp.where(hit, float(r), rank)
        s = jnp.where(hit, -jnp.inf, s)
    taken = jnp.sum(jnp.where(s == s0, 0.0, 1.0), axis=0, keepdims=True)
    rows = jnp.concatenate(idxs, axis=0) if exact else None
    return jnp.concatenate(vals, axis=0), rows, rank, s, taken


def _mix_route_kernel(tiles_per_seq, x_ref, oa_ref, gb_ref, gc_ref, hc_ref, gcp_ref, hcp_ref,
                      cw_ref, cb_ref, ag_ref, cg_ref, wo_ref, fg_ref, wq_ref, keys_ref,
                      x1_ref, xnt_ref, cnt_ref, e1_ref, rk2_ref, e2_ref, q_sc):
    i = pl.program_id(0)
    tm = x_ref.shape[0]
    u = gc_ref[...] * hc_ref[...]
    first = (i % tiles_per_seq) == 0
    u_prev = jnp.where(first, 0.0, gcp_ref[...] * hcp_ref[...])
    u_ext = jnp.concatenate([u_prev, u], axis=0)
    cw = cw_ref[...]
    y = (cw[0:1, :] * u_ext[SUBLANES - 2:SUBLANES - 2 + tm, :]
         + cw[1:2, :] * u_ext[SUBLANES - 1:SUBLANES - 1 + tm, :]
         + cw[2:3, :] * u) + cb_ref[...]
    o_conv = gb_ref[...] * y
    mixed = jnp.concatenate([_rms(oa_ref[...], ag_ref[...]), _rms(o_conv, cg_ref[...])], axis=1)
    x1 = x_ref[...] + jnp.dot(mixed.astype(BF16), wo_ref[...], preferred_element_type=F32)
    x1_ref[...] = x1
    xn = _rms(x1, fg_ref[...]).astype(BF16)
    xnt_ref[...] = xn.T
    q_sc[...] = jnp.dot(xn, wq_ref[...], preferred_element_type=F32)

    def head(h, carry):
        c0 = pl.multiple_of(h * PEER_DK, PEER_DK)
        q1 = q_sc[:, pl.ds(c0, D_HALF)].astype(BF16)
        q2 = q_sc[:, pl.ds(c0 + D_HALF, D_HALF)].astype(BF16)
        nt = (((1,), (1,)), ((), ()))
        s1 = lax.dot_general(keys_ref[0, h], q1, nt, preferred_element_type=F32)
        s2 = lax.dot_general(keys_ref[1, h], q2, nt, preferred_element_type=F32)

        def route(exact):
            v1, i1, _, _, took1 = _extract_topk(s1, PEER_TOPK, False, exact)
            v2, _, rk2, _, took2 = _extract_topk(s2, PEER_TOPK, True, exact)
            widths = [PEER_TOPK // (a + 1) for a in range(PEER_TOPK)]
            pad = -sum(widths) % SUBLANES
            cand = jnp.concatenate(
                [v1[a:a + 1, :] + v2[0:widths[a], :] for a in range(PEER_TOPK)]
                + [jnp.full((pad, tm), -jnp.inf, F32)], axis=0)
            best, _, _, left, took = _extract_topk(cand, PEER_TOPK, False, exact)
            picked = jnp.where(left == cand, 0.0, 1.0)
            z = jnp.sum(jnp.exp(best - best[0:1, :]), axis=0, keepdims=True)
            cnt = jnp.zeros_like(s1)
            key_row = lax.broadcasted_iota(jnp.int32, s1.shape, 0).astype(F32) if exact else None
            off = 0
            for a in range(PEER_TOPK):
                n_a = jnp.sum(picked[off:off + widths[a], :], axis=0, keepdims=True)
                at_a = key_row == i1[a:a + 1, :] if exact else s1 == v1[a:a + 1, :]
                cnt = jnp.where(at_a, n_a, cnt)
                off += widths[a]
            cnt_ref[h] = cnt
            e1_ref[h] = jnp.exp(s1 - v1[0:1, :])
            rk2_ref[h] = rk2.astype(BF16)
            e2_ref[h] = (jnp.exp(s2 - v2[0:1, :]) * (0.5 / z)).astype(BF16)
            k = float(PEER_TOPK)
            return jnp.where((took1 == k) & (took2 == k) & (took == k), 0.0, 1.0)

        tied = route(exact=False)

        @pl.when(jnp.max(tied) > 0.0)
        def _():
            route(exact=True)

        return carry

    lax.fori_loop(0, PEER_HEADS, head, 0)


def _mix_route(x2, oa, cv, conv_w, conv_b, attn_g, conv_g, wo_bf16, ffn_g, wq_bf16, keys_bf16, seq):
    m = x2.shape[0]
    tm = TM_MIX
    row = lambda i: (i, 0)
    const2 = lambda i: (0, 0)
    halo = tm // SUBLANES
    return pl.pallas_call(
        functools.partial(_mix_route_kernel, seq // tm),
        grid=(m // tm,),
        in_specs=[
            pl.BlockSpec((tm, D_MODEL), row),
            pl.BlockSpec((tm, SB_WIDTH), row),
            pl.BlockSpec((tm, CONV_WIDTH), lambda i: (i, 0)),
            pl.BlockSpec((tm, CONV_WIDTH), lambda i: (i, 1)),
            pl.BlockSpec((tm, CONV_WIDTH), lambda i: (i, 2)),
            pl.BlockSpec((SUBLANES, CONV_WIDTH), lambda i: (jnp.maximum(i * halo - 1, 0), 1)),
            pl.BlockSpec((SUBLANES, CONV_WIDTH), lambda i: (jnp.maximum(i * halo - 1, 0), 2)),
            pl.BlockSpec((CONV_K, CONV_WIDTH), const2),
            pl.BlockSpec((1, CONV_WIDTH), const2),
            pl.BlockSpec((1, SB_WIDTH), const2),
            pl.BlockSpec((1, CONV_WIDTH), const2),
            pl.BlockSpec((D_MODEL, D_MODEL), const2),
            pl.BlockSpec((1, D_MODEL), const2),
            pl.BlockSpec((D_MODEL, PEER_HEADS * PEER_DK), const2),
            pl.BlockSpec((2, PEER_HEADS, N_KEYS, D_HALF), lambda i: (0, 0, 0, 0)),
        ],
        out_specs=[
            pl.BlockSpec((tm, D_MODEL), row),
            pl.BlockSpec((D_MODEL, tm), lambda i: (0, i)),
        ] + [pl.BlockSpec((PEER_HEADS, N_KEYS, tm), lambda i: (0, 0, i))] * 4,
        out_shape=[
            jax.ShapeDtypeStruct((m, D_MODEL), F32),
            jax.ShapeDtypeStruct((D_MODEL, m), BF16),
        ] + [jax.ShapeDtypeStruct((PEER_HEADS, N_KEYS, m), dt) for dt in (F32, F32, BF16, BF16)],
        scratch_shapes=[pltpu.VMEM((tm, PEER_HEADS * PEER_DK), F32)],
        compiler_params=pltpu.CompilerParams(
            dimension_semantics=("parallel",), vmem_limit_bytes=VMEM_LIMIT),
        name="mix_route",
    )(x2, oa, cv, cv, cv, cv, cv, conv_w, conv_b.reshape(1, -1), attn_g.reshape(1, -1),
      conv_g.reshape(1, -1), wo_bf16, ffn_g.reshape(1, -1), wq_bf16, keys_bf16)


def _peer_kernel(xnt_ref, u_ref, vt_ref, cnt_ref, e1_ref, rk2_ref, e2_ref, x1_ref, fg_ref,
                 out_ref, acc_sc):
    e = pl.program_id(1)

    @pl.when(e == 0)
    def _():
        acc_sc[...] = jnp.zeros_like(acc_sc)

    xnt = xnt_ref[...]
    rows = TE_PEER // N_KEYS
    w_parts = []
    for r in range(rows):
        sl = slice(r * N_KEYS, (r + 1) * N_KEYS)
        act = jnp.dot(u_ref[sl, :], xnt, preferred_element_type=F32)
        tiles = (N_KEYS // BF16_ROWS, BF16_ROWS, act.shape[1])
        gate = jnp.zeros(tiles, BF16)
        for h in range(PEER_HEADS):
            cnt_t = jnp.broadcast_to(cnt_ref[h, r:r + 1, :], tiles[1:]).astype(BF16)
            e1_t = jnp.broadcast_to(e1_ref[h, r:r + 1, :], tiles[1:]).astype(BF16)
            picked = rk2_ref[h].reshape(tiles) < cnt_t[None]
            gate = gate + jnp.where(picked, e2_ref[h].reshape(tiles) * e1_t[None], 0.0)
        a_b = act.astype(BF16)
        inner = a_b * (GELU_C0 + (GELU_C0 * GELU_C1) * (a_b * a_b))
        w_parts.append(a_b * (1.0 + jnp.tanh(inner)) * gate.reshape(act.shape))
    w = jnp.concatenate(w_parts, axis=0)
    acc_sc[...] += jnp.dot(vt_ref[...], w, preferred_element_type=F32)

    @pl.when(e == pl.num_programs(1) - 1)
    def _():
        out_ref[...] = _rms(x1_ref[...] + acc_sc[...].T, fg_ref[...])


def _peer_experts(xnt, u_bf16, vt_bf16, cnt, e1, rk2, e2, x1, final_g):
    m = xnt.shape[1]
    n_exp = u_bf16.shape[0]
    tm, te = TM_PEER, TE_PEER
    rows = te // N_KEYS
    per_row = pl.BlockSpec((PEER_HEADS, rows, tm), lambda t, e: (0, e, t))
    per_key = pl.BlockSpec((PEER_HEADS, N_KEYS, tm), lambda t, e: (0, 0, t))
    return pl.pallas_call(
        _peer_kernel,
        grid=(m // tm, n_exp // te),
        in_specs=[
            pl.BlockSpec((D_MODEL, tm), lambda t, e: (0, t)),
            pl.BlockSpec((te, D_MODEL), lambda t, e: (e, 0)),
            pl.BlockSpec((D_MODEL, te), lambda t, e: (0, e)),
            per_row, per_row, per_key, per_key,
            pl.BlockSpec((tm, D_MODEL), lambda t, e: (t, 0)),
            pl.BlockSpec((1, D_MODEL), lambda t, e: (0, 0)),
        ],
        out_specs=pl.BlockSpec((tm, D_MODEL), lambda t, e: (t, 0)),
        out_shape=jax.ShapeDtypeStruct((m, D_MODEL), F32),
        scratch_shapes=[pltpu.VMEM((D_MODEL, tm), F32)],
        compiler_params=pltpu.CompilerParams(
            dimension_semantics=("parallel", "arbitrary"), vmem_limit_bytes=VMEM_LIMIT),
        name="peer_experts",
    )(xnt, u_bf16, vt_bf16, cnt, e1, rk2, e2, x1, final_g.reshape(1, D_MODEL))


def kernel(x, mix_norm_g, w_in, conv_w, conv_b, attn_out_g, conv_out_g, w_out, ffn_norm_g,
           peer_wq, peer_keys, peer_u, peer_v, final_norm_g):
    b, s, d = x.shape
    assert w_in.shape[0] == 1, "single-layer block"
    w_in_b = w_in[0].astype(BF16)
    w_out_b = w_out[0].astype(BF16)
    wq_b = peer_wq[0].astype(BF16)
    keys_b = peer_keys[0].astype(BF16)
    u_b = peer_u[0].astype(BF16)
    vt_b = peer_v[0].T.astype(BF16)
    m = b * s
    x2 = x.reshape(m, d)
    qkv, cv = _inproj(x2, mix_norm_g[0], w_in_b)
    oa = _attention(qkv.reshape(b, s, 3 * SB_WIDTH)).reshape(m, SB_WIDTH)
    x1, xnt, cnt, e1, rk2, e2 = _mix_route(
        x2, oa, cv, conv_w[0], conv_b[0], attn_out_g[0], conv_out_g[0],
        w_out_b, ffn_norm_g[0], wq_b, keys_b, s)
    out = _peer_experts(xnt, u_b, vt_b, cnt, e1, rk2, e2, x1, final_norm_g)
    return out.reshape(b, s, d)
```

```python
import functools
import math

import jax
import jax.numpy as jnp
from jax import lax
from jax.experimental import pallas as pl
from jax.experimental.pallas import tpu as pltpu

F32 = jnp.float32
BF16 = jnp.bfloat16

D_MODEL = 1024
SB_HEADS = 8
HEAD_DIM = 64
SB_WIDTH = SB_HEADS * HEAD_DIM
CONV_WIDTH = D_MODEL - SB_WIDTH
CONV_K = 3
N_KEYS = 128
PEER_HEADS = 8
PEER_DK = 256
D_HALF = PEER_DK // 2
PEER_TOPK = 16
EPS = 1e-6
GELU_C0 = math.sqrt(2.0 / math.pi)
GELU_C1 = 0.044715

SUBLANES = 8
LANES = 128
BF16_ROWS = 16
V7X_VMEM_BYTES = 64 * 1024 * 1024

TM_PROJ = 512
TQ = 512
HEADS_PER_STEP = 8
TK = 256
TM_MIX = 512
TM_PEER = 512
TE_PEER = 2048
VMEM_LIMIT = V7X_VMEM_BYTES * 7 // 8


def _rms(x, g):
    return x * lax.rsqrt(jnp.mean(x * x, axis=-1, keepdims=True) + EPS) * g


def _inproj_kernel(x_ref, g_ref, w_ref, qkv_ref, cv_ref):
    h = _rms(x_ref[...], g_ref[...])
    p = jnp.dot(h.astype(BF16), w_ref[...], preferred_element_type=F32)
    qkv_ref[...] = p[:, : 3 * SB_WIDTH].astype(BF16)
    cv_ref[...] = p[:, 3 * SB_WIDTH:]


def _inproj(x2, g, w_bf16):
    m = x2.shape[0]
    n = w_bf16.shape[1]
    return pl.pallas_call(
        _inproj_kernel,
        grid=(m // TM_PROJ,),
        in_specs=[
            pl.BlockSpec((TM_PROJ, D_MODEL), lambda i: (i, 0)),
            pl.BlockSpec((1, D_MODEL), lambda i: (0, 0)),
            pl.BlockSpec((D_MODEL, n), lambda i: (0, 0)),
        ],
        out_specs=[
            pl.BlockSpec((TM_PROJ, 3 * SB_WIDTH), lambda i: (i, 0)),
            pl.BlockSpec((TM_PROJ, 3 * CONV_WIDTH), lambda i: (i, 0)),
        ],
        out_shape=[
            jax.ShapeDtypeStruct((m, 3 * SB_WIDTH), BF16),
            jax.ShapeDtypeStruct((m, 3 * CONV_WIDTH), F32),
        ],
        compiler_params=pltpu.CompilerParams(
            dimension_semantics=("parallel",), vmem_limit_bytes=VMEM_LIMIT),
        name="inproj",
    )(x2, g.reshape(1, D_MODEL), w_bf16)


def _softplus(z):
    return jnp.maximum(z, 0.0) + jnp.log(1.0 + jnp.exp(-jnp.abs(z)))


def _attn_kernel(q_ref, k_ref, v_ref, o_ref):
    i = pl.program_id(2)
    scale = 1.0 / math.sqrt(HEAD_DIM)
    kr = lax.broadcasted_iota(jnp.int32, (TK, TK), 0)
    kc = lax.broadcasted_iota(jnp.int32, (TK, TK), 1)
    tri = (kr >= kc).astype(BF16)
    rows = lax.broadcasted_iota(jnp.int32, (TQ, TK), 0)
    cols = lax.broadcasted_iota(jnp.int32, (TQ, TK), 1)
    kb = TQ // TK
    heads = [slice(hh * HEAD_DIM, (hh + 1) * HEAD_DIM) for hh in range(HEADS_PER_STEP)]
    qs = [q_ref[:, sl] * scale for sl in heads]
    nt = (((1,), (1,)), ((), ()))

    def block(s0, carry, mask):
        zs = [lax.dot_general(q, k_ref[pl.ds(s0, TK), sl], nt, preferred_element_type=F32)
              for q, sl in zip(qs, heads)]
        ls = [-_softplus(z) for z in zs]
        if mask is not None:
            ls = [jnp.where(mask, l, 0.0) for l in ls]
        his = [l.astype(BF16) for l in ls]
        los = [(l - hi.astype(F32)).astype(BF16) for l, hi in zip(ls, his)]
        cs = [jnp.dot(hi, tri, preferred_element_type=F32)
              + jnp.dot(lo, tri, preferred_element_type=F32) + r
              for hi, lo, (_, r) in zip(his, los, carry)]
        ps = [jnp.exp(z + c) for z, c in zip(zs, cs)]
        if mask is not None:
            ps = [jnp.where(mask, p, 0.0) for p in ps]
        os = [o + jnp.dot(p.astype(BF16), v_ref[pl.ds(s0, TK), sl], preferred_element_type=F32)
              for p, sl, (o, _) in zip(ps, heads, carry)]
        return tuple((o, c[:, 0:1]) for o, c in zip(os, cs))

    carry = tuple((jnp.zeros((TQ, HEAD_DIM), F32), jnp.zeros((TQ, 1), F32)) for _ in heads)
    for d in reversed(range(kb)):
        carry = block(pl.multiple_of(i * TQ + d * TK, TK), carry, rows > cols + d * TK)

    def body(jj, carry):
        return block(pl.multiple_of((i * kb - 1 - jj) * TK, TK), carry, None)

    carry = lax.fori_loop(0, i * kb, body, carry)
    o_ref[...] = jnp.concatenate([o for o, _ in carry], axis=1)


def _attention(qkv3):
    b, s, _ = qkv3.shape
    hp = SB_HEADS // HEADS_PER_STEP
    hw = HEADS_PER_STEP * HEAD_DIM
    return pl.pallas_call(
        _attn_kernel,
        grid=(b, hp, s // TQ),
        in_specs=[
            pl.BlockSpec((None, TQ, hw), lambda bi, h, i: (bi, i, h)),
            pl.BlockSpec((None, s, hw), lambda bi, h, i: (bi, 0, hp + h)),
            pl.BlockSpec((None, s, hw), lambda bi, h, i: (bi, 0, 2 * hp + h)),
        ],
        out_specs=pl.BlockSpec((None, TQ, hw), lambda bi, h, i: (bi, i, h)),
        out_shape=jax.ShapeDtypeStruct((b, s, SB_WIDTH), F32),
        compiler_params=pltpu.CompilerParams(
            dimension_semantics=("parallel", "parallel", "arbitrary"),
            vmem_limit_bytes=VMEM_LIMIT),
        name="sb_attention",
    )(qkv3, qkv3, qkv3)


def _extract_topk(s, k, want_rank, exact):
    n = s.shape[0]
    s0 = s
    iota = lax.broadcasted_iota(jnp.int32, s.shape, 0).astype(F32) if exact else None
    rank = jnp.full(s.shape, float(k), F32) if want_rank else None
    vals, idxs = [], []
    for r in range(k):
        m = jnp.max(s, axis=0, keepdims=True)
        if exact:
            im = jnp.min(jnp.where(s == m, iota, float(n)), axis=0, keepdims=True)
            hit = iota == im
            idxs.append(im)
        else:
            hit = s == m
        vals.append(m)
        if want_rank:
            rank = jnp.where(hit, float(r), rank)
        s = jnp.where(hit, -jnp.inf, s)
    taken = jnp.sum(jnp.where(s == s0, 0.0, 1.0), axis=0, keepdims=True)
    rows = jnp.concatenate(idxs, axis=0) if exact else None
    return jnp.concatenate(vals, axis=0), rows, rank, s, taken


def _mix_route_kernel(tiles_per_seq, x_ref, oa_ref, gb_ref, gc_ref, hc_ref, gcp_ref, hcp_ref,
                      cw_ref, cb_ref, ag_ref, cg_ref, wo_ref, fg_ref, wq_ref, keys_ref,
                      x1_ref, xnt_ref, cnt_ref, e1_ref, rk2_ref, e2_ref, q_sc):
    i = pl.program_id(0)
    tm = x_ref.shape[0]
    u = gc_ref[...] * hc_ref[...]
    first = (i % tiles_per_seq) == 0
    u_prev = jnp.where(first, 0.0, gcp_ref[...] * hcp_ref[...])
    u_ext = jnp.concatenate([u_prev, u], axis=0)
    cw = cw_ref[...]
    y = (cw[0:1, :] * u_ext[SUBLANES - 2:SUBLANES - 2 + tm, :]
         + cw[1:2, :] * u_ext[SUBLANES - 1:SUBLANES - 1 + tm, :]
         + cw[2:3, :] * u) + cb_ref[...]
    o_conv = gb_ref[...] * y
    mixed = jnp.concatenate([_rms(oa_ref[...], ag_ref[...]), _rms(o_conv, cg_ref[...])], axis=1)
    x1 = x_ref[...] + jnp.dot(mixed.astype(BF16), wo_ref[...], preferred_element_type=F32)
    x1_ref[...] = x1
    xn = _rms(x1, fg_ref[...]).astype(BF16)
    xnt_ref[...] = xn.T
    q_sc[...] = jnp.dot(xn, wq_ref[...], preferred_element_type=F32)

    def head(h, carry):
        c0 = pl.multiple_of(h * PEER_DK, PEER_DK)
        q1 = q_sc[:, pl.ds(c0, D_HALF)].astype(BF16)
        q2 = q_sc[:, pl.ds(c0 + D_HALF, D_HALF)].astype(BF16)
        nt = (((1,), (1,)), ((), ()))
        s1 = lax.dot_general(keys_ref[0, h], q1, nt, preferred_element_type=F32)
        s2 = lax.dot_general(keys_ref[1, h], q2, nt, preferred_element_type=F32)

        def route(exact):
            v1, i1, _, _, took1 = _extract_topk(s1, PEER_TOPK, False, exact)
            v2, _, rk2, _, took2 = _extract_topk(s2, PEER_TOPK, True, exact)
            widths = [PEER_TOPK // (a + 1) for a in range(PEER_TOPK)]
            pad = -sum(widths) % SUBLANES
            cand = jnp.concatenate(
                [v1[a:a + 1, :] + v2[0:widths[a], :] for a in range(PEER_TOPK)]
                + [jnp.full((pad, tm), -jnp.inf, F32)], axis=0)
            best, _, _, left, took = _extract_topk(cand, PEER_TOPK, False, exact)
            picked = jnp.where(left == cand, 0.0, 1.0)
            z = jnp.sum(jnp.exp(best - best[0:1, :]), axis=0, keepdims=True)
            cnt = jnp.zeros_like(s1)
            key_row = lax.broadcasted_iota(jnp.int32, s1.shape, 0).astype(F32) if exact else None
            off = 0
            for a in range(PEER_TOPK):
                n_a = jnp.sum(picked[off:off + widths[a], :], axis=0, keepdims=True)
                at_a = key_row == i1[a:a + 1, :] if exact else s1 == v1[a:a + 1, :]
                cnt = jnp.where(at_a, n_a, cnt)
                off += widths[a]
            for c in range(tm // LANES):
                cnt_ref[c, h] = cnt[:, c * LANES:(c + 1) * LANES]
            e1 = jnp.exp(s1 - v1[0:1, :])
            for c in range(tm // LANES):
                e1_ref[c, h] = e1[:, c * LANES:(c + 1) * LANES]
            rk2_ref[h] = rk2.astype(BF16)
            e2_ref[h] = (jnp.exp(s2 - v2[0:1, :]) * (0.5 / z)).astype(BF16)
            k = float(PEER_TOPK)
            return jnp.where((took1 == k) & (took2 == k) & (took == k), 0.0, 1.0)

        tied = route(exact=False)

        @pl.when(jnp.max(tied) > 0.0)
        def _():
            route(exact=True)

        return carry

    lax.fori_loop(0, PEER_HEADS, head, 0)


def _mix_route(x2, oa, cv, conv_w, conv_b, attn_g, conv_g, wo_bf16, ffn_g, wq_bf16, keys_bf16, seq):
    m = x2.shape[0]
    tm = TM_MIX
    row = lambda i: (i, 0)
    const2 = lambda i: (0, 0)
    halo = tm // SUBLANES
    return pl.pallas_call(
        functools.partial(_mix_route_kernel, seq // tm),
        grid=(m // tm,),
        in_specs=[
            pl.BlockSpec((tm, D_MODEL), row),
            pl.BlockSpec((tm, SB_WIDTH), row),
            pl.BlockSpec((tm, CONV_WIDTH), lambda i: (i, 0)),
            pl.BlockSpec((tm, CONV_WIDTH), lambda i: (i, 1)),
            pl.BlockSpec((tm, CONV_WIDTH), lambda i: (i, 2)),
            pl.BlockSpec((SUBLANES, CONV_WIDTH), lambda i: (jnp.maximum(i * halo - 1, 0), 1)),
            pl.BlockSpec((SUBLANES, CONV_WIDTH), lambda i: (jnp.maximum(i * halo - 1, 0), 2)),
            pl.BlockSpec((CONV_K, CONV_WIDTH), const2),
            pl.BlockSpec((1, CONV_WIDTH), const2),
            pl.BlockSpec((1, SB_WIDTH), const2),
            pl.BlockSpec((1, CONV_WIDTH), const2),
            pl.BlockSpec((D_MODEL, D_MODEL), const2),
            pl.BlockSpec((1, D_MODEL), const2),
            pl.BlockSpec((D_MODEL, PEER_HEADS * PEER_DK), const2),
            pl.BlockSpec((2, PEER_HEADS, N_KEYS, D_HALF), lambda i: (0, 0, 0, 0)),
        ],
        out_specs=[
            pl.BlockSpec((tm, D_MODEL), row),
            pl.BlockSpec((D_MODEL, tm), lambda i: (0, i)),
        ] + [pl.BlockSpec((tm // LANES, PEER_HEADS, N_KEYS, LANES), lambda i: (i, 0, 0, 0))] * 2
        + [pl.BlockSpec((PEER_HEADS, N_KEYS, tm), lambda i: (0, 0, i))] * 2,
        out_shape=[
            jax.ShapeDtypeStruct((m, D_MODEL), F32),
            jax.ShapeDtypeStruct((D_MODEL, m), BF16),
        ] + [jax.ShapeDtypeStruct((m // LANES, PEER_HEADS, N_KEYS, LANES), F32)] * 2
        + [jax.ShapeDtypeStruct((PEER_HEADS, N_KEYS, m), BF16)] * 2,
        scratch_shapes=[pltpu.VMEM((tm, PEER_HEADS * PEER_DK), F32)],
        compiler_params=pltpu.CompilerParams(
            dimension_semantics=("parallel",), vmem_limit_bytes=VMEM_LIMIT),
        name="mix_route",
    )(x2, oa, cv, cv, cv, cv, cv, conv_w, conv_b.reshape(1, -1), attn_g.reshape(1, -1),
      conv_g.reshape(1, -1), wo_bf16, ffn_g.reshape(1, -1), wq_bf16, keys_bf16)


def _peer_kernel(xnt_ref, u_ref, vt_ref, cnt_ref, e1_ref, rk2_ref, e2_ref, x1_ref, fg_ref,
                 out_ref, acc_sc):
    e = pl.program_id(1)

    @pl.when(e == 0)
    def _():
        acc_sc[...] = jnp.zeros_like(acc_sc)

    xnt = xnt_ref[...]
    rows = TE_PEER // N_KEYS
    w_parts = []
    for r in range(rows):
        sl = slice(r * N_KEYS, (r + 1) * N_KEYS)
        act = jnp.dot(u_ref[sl, :], xnt, preferred_element_type=F32)
        tiles = (N_KEYS // BF16_ROWS, BF16_ROWS, act.shape[1])
        gate = jnp.zeros(tiles, BF16)
        for h in range(PEER_HEADS):
            lane_tiles = range(act.shape[1] // LANES)
            row = pl.ds(r, BF16_ROWS, stride=0)
            cnt_t = jnp.concatenate([cnt_ref[c, h, row, :] for c in lane_tiles], axis=1).astype(BF16)
            e1_t = jnp.concatenate([e1_ref[c, h, row, :] for c in lane_tiles], axis=1).astype(BF16)
            picked = rk2_ref[h].reshape(tiles) < cnt_t[None]
            gate = gate + jnp.where(picked, e2_ref[h].reshape(tiles) * e1_t[None], 0.0)
        a_b = act.astype(BF16)
        inner = a_b * (GELU_C0 + (GELU_C0 * GELU_C1) * (a_b * a_b))
        w_parts.append(a_b * (1.0 + jnp.tanh(inner)) * gate.reshape(act.shape))
    w = jnp.concatenate(w_parts, axis=0)
    acc_sc[...] += jnp.dot(vt_ref[...], w, preferred_element_type=F32)

    @pl.when(e == pl.num_programs(1) - 1)
    def _():
        out_ref[...] = _rms(x1_ref[...] + acc_sc[...].T, fg_ref[...])


def _peer_experts(xnt, u_bf16, vt_bf16, cnt, e1, rk2, e2, x1, final_g):
    m = xnt.shape[1]
    n_exp = u_bf16.shape[0]
    tm, te = TM_PEER, TE_PEER
    rows = te // N_KEYS
    per_row = pl.BlockSpec((tm // LANES, PEER_HEADS, rows, LANES), lambda t, e: (t, 0, e, 0))
    per_key = pl.BlockSpec((PEER_HEADS, N_KEYS, tm), lambda t, e: (0, 0, t))
    return pl.pallas_call(
        _peer_kernel,
        grid=(m // tm, n_exp // te),
        in_specs=[
            pl.BlockSpec((D_MODEL, tm), lambda t, e: (0, t)),
            pl.BlockSpec((te, D_MODEL), lambda t, e: (e, 0)),
            pl.BlockSpec((D_MODEL, te), lambda t, e: (0, e)),
            per_row, per_row, per_key, per_key,
            pl.BlockSpec((tm, D_MODEL), lambda t, e: (t, 0)),
            pl.BlockSpec((1, D_MODEL), lambda t, e: (0, 0)),
        ],
        out_specs=pl.BlockSpec((tm, D_MODEL), lambda t, e: (t, 0)),
        out_shape=jax.ShapeDtypeStruct((m, D_MODEL), F32),
        scratch_shapes=[pltpu.VMEM((D_MODEL, tm), F32)],
        compiler_params=pltpu.CompilerParams(
            dimension_semantics=("parallel", "arbitrary"), vmem_limit_bytes=VMEM_LIMIT),
        name="peer_experts",
    )(xnt, u_bf16, vt_bf16, cnt, e1, rk2, e2, x1, final_g.reshape(1, D_MODEL))


def kernel(x, mix_norm_g, w_in, conv_w, conv_b, attn_out_g, conv_out_g, w_out, ffn_norm_g,
           peer_wq, peer_keys, peer_u, peer_v, final_norm_g):
    b, s, d = x.shape
    assert w_in.shape[0] == 1, "single-layer block"
    w_in_b = w_in[0].astype(BF16)
    w_out_b = w_out[0].astype(BF16)
    wq_b = peer_wq[0].astype(BF16)
    keys_b = peer_keys[0].astype(BF16)
    u_b = peer_u[0].astype(BF16)
    vt_b = peer_v[0].T.astype(BF16)
    m = b * s
    x2 = x.reshape(m, d)
    qkv, cv = _inproj(x2, mix_norm_g[0], w_in_b)
    oa = _attention(qkv.reshape(b, s, 3 * SB_WIDTH)).reshape(m, SB_WIDTH)
    x1, xnt, cnt, e1, rk2, e2 = _mix_route(
        x2, oa, cv, conv_w[0], conv_b[0], attn_out_g[0], conv_out_g[0],
        w_out_b, ffn_norm_g[0], wq_b, keys_b, s)
    out = _peer_experts(xnt, u_b, vt_b, cnt, e1, rk2, e2, x1, final_norm_g)
    return out.reshape(b, s, d)
```

```python
import functools
import math

import jax
import jax.numpy as jnp
from jax import lax
from jax.experimental import pallas as pl
from jax.experimental.pallas import tpu as pltpu

F32 = jnp.float32
BF16 = jnp.bfloat16

D_MODEL = 1024
SB_HEADS = 8
HEAD_DIM = 64
SB_WIDTH = SB_HEADS * HEAD_DIM
CONV_WIDTH = D_MODEL - SB_WIDTH
CONV_K = 3
N_KEYS = 128
PEER_HEADS = 8
PEER_DK = 256
D_HALF = PEER_DK // 2
PEER_TOPK = 16
EPS = 1e-6
GELU_C0 = math.sqrt(2.0 / math.pi)
GELU_C1 = 0.044715

SUBLANES = 8
LANES = 128
BF16_ROWS = 16
V7X_VMEM_BYTES = 64 * 1024 * 1024

TM_PROJ = 512
TQ = 512
HEADS_PER_STEP = 8
TK = 256
TM_MIX = 512
TM_PEER = 512
TE_PEER = 2048
VMEM_LIMIT = V7X_VMEM_BYTES * 7 // 8


def _rms(x, g):
    return x * lax.rsqrt(jnp.mean(x * x, axis=-1, keepdims=True) + EPS) * g


def _inproj_kernel(x_ref, g_ref, w_ref, qkv_ref, cv_ref):
    h = _rms(x_ref[...], g_ref[...])
    p = jnp.dot(h.astype(BF16), w_ref[...], preferred_element_type=F32)
    qkv_ref[...] = p[:, : 3 * SB_WIDTH].astype(BF16)
    cv_ref[...] = p[:, 3 * SB_WIDTH:]


def _inproj(x2, g, w_bf16):
    m = x2.shape[0]
    n = w_bf16.shape[1]
    return pl.pallas_call(
        _inproj_kernel,
        grid=(m // TM_PROJ,),
        in_specs=[
            pl.BlockSpec((TM_PROJ, D_MODEL), lambda i: (i, 0)),
            pl.BlockSpec((1, D_MODEL), lambda i: (0, 0)),
            pl.BlockSpec((D_MODEL, n), lambda i: (0, 0)),
        ],
        out_specs=[
            pl.BlockSpec((TM_PROJ, 3 * SB_WIDTH), lambda i: (i, 0)),
            pl.BlockSpec((TM_PROJ, 3 * CONV_WIDTH), lambda i: (i, 0)),
        ],
        out_shape=[
            jax.ShapeDtypeStruct((m, 3 * SB_WIDTH), BF16),
            jax.ShapeDtypeStruct((m, 3 * CONV_WIDTH), F32),
        ],
        compiler_params=pltpu.CompilerParams(
            dimension_semantics=("parallel",), vmem_limit_bytes=VMEM_LIMIT),
        name="inproj",
    )(x2, g.reshape(1, D_MODEL), w_bf16)


def _softplus(z):
    return jnp.maximum(z, 0.0) + jnp.log(1.0 + jnp.exp(-jnp.abs(z)))


def _attn_kernel(q_ref, k_ref, v_ref, o_ref):
    i = pl.program_id(2)
    scale = 1.0 / math.sqrt(HEAD_DIM)
    kr = lax.broadcasted_iota(jnp.int32, (TK, TK), 0)
    kc = lax.broadcasted_iota(jnp.int32, (TK, TK), 1)
    tri = (kr >= kc).astype(BF16)
    rows = lax.broadcasted_iota(jnp.int32, (TQ, TK), 0)
    cols = lax.broadcasted_iota(jnp.int32, (TQ, TK), 1)
    kb = TQ // TK
    heads = [slice(hh * HEAD_DIM, (hh + 1) * HEAD_DIM) for hh in range(HEADS_PER_STEP)]
    qs = [q_ref[:, sl] * scale for sl in heads]
    nt = (((1,), (1,)), ((), ()))

    def block(s0, carry, mask):
        zs = [lax.dot_general(q, k_ref[pl.ds(s0, TK), sl], nt, preferred_element_type=F32)
              for q, sl in zip(qs, heads)]
        ls = [-_softplus(z) for z in zs]
        if mask is not None:
            ls = [jnp.where(mask, l, 0.0) for l in ls]
        his = [l.astype(BF16) for l in ls]
        los = [(l - hi.astype(F32)).astype(BF16) for l, hi in zip(ls, his)]
        cs = [jnp.dot(hi, tri, preferred_element_type=F32)
              + jnp.dot(lo, tri, preferred_element_type=F32) + r
              for hi, lo, (_, r) in zip(his, los, carry)]
        ps = [jnp.exp(z + c) for z, c in zip(zs, cs)]
        if mask is not None:
            ps = [jnp.where(mask, p, 0.0) for p in ps]
        os = [o + jnp.dot(p.astype(BF16), v_ref[pl.ds(s0, TK), sl], preferred_element_type=F32)
              for p, sl, (o, _) in zip(ps, heads, carry)]
        return tuple((o, c[:, 0:1]) for o, c in zip(os, cs))

    carry = tuple((jnp.zeros((TQ, HEAD_DIM), F32), jnp.zeros((TQ, 1), F32)) for _ in heads)
    for d in reversed(range(kb)):
        carry = block(pl.multiple_of(i * TQ + d * TK, TK), carry, rows > cols + d * TK)

    def body(jj, carry):
        return block(pl.multiple_of((i * kb - 1 - jj) * TK, TK), carry, None)

    carry = lax.fori_loop(0, i * kb, body, carry)
    o_ref[...] = jnp.concatenate([o for o, _ in carry], axis=1)


def _attention(qkv3):
    b, s, _ = qkv3.shape
    hp = SB_HEADS // HEADS_PER_STEP
    hw = HEADS_PER_STEP * HEAD_DIM
    return pl.pallas_call(
        _attn_kernel,
        grid=(b, hp, s // TQ),
        in_specs=[
            pl.BlockSpec((None, TQ, hw), lambda bi, h, i: (bi, i, h)),
            pl.BlockSpec((None, s, hw), lambda bi, h, i: (bi, 0, hp + h)),
            pl.BlockSpec((None, s, hw), lambda bi, h, i: (bi, 0, 2 * hp + h)),
        ],
        out_specs=pl.BlockSpec((None, TQ, hw), lambda bi, h, i: (bi, i, h)),
        out_shape=jax.ShapeDtypeStruct((b, s, SB_WIDTH), F32),
        compiler_params=pltpu.CompilerParams(
            dimension_semantics=("parallel", "parallel", "arbitrary"),
            vmem_limit_bytes=VMEM_LIMIT),
        name="sb_attention",
    )(qkv3, qkv3, qkv3)


def _extract_topk(s, k, want_rank, exact):
    n = s.shape[0]
    s0 = s
    iota = lax.broadcasted_iota(jnp.int32, s.shape, 0).astype(F32) if exact else None
    rank = jnp.full(s.shape, float(k), F32) if want_rank else None
    vals, idxs = [], []
    for r in range(k):
        m = jnp.max(s, axis=0, keepdims=True)
        if exact:
            im = jnp.min(jnp.where(s == m, iota, float(n)), axis=0, keepdims=True)
            hit = iota == im
            idxs.append(im)
        else:
            hit = s == m
        vals.append(m)
        if want_rank:
            rank = jnp.where(hit, float(r), rank)
        s = jnp.where(hit, -jnp.inf, s)
    taken = jnp.sum(jnp.where(s == s0, 0.0, 1.0), axis=0, keepdims=True)
    rows = jnp.concatenate(idxs, axis=0) if exact else None
    return jnp.concatenate(vals, axis=0), rows, rank, s, taken


def _mix_route_kernel(tiles_per_seq, x_ref, oa_ref, gb_ref, gc_ref, hc_ref, gcp_ref, hcp_ref,
                      cw_ref, cb_ref, ag_ref, cg_ref, wo_ref, fg_ref, wq_ref, keys_ref,
                      uf_ref, vf_ref,
                      x1_ref, xnt_ref, cnt_ref, e1_ref, rk2_ref, e2_ref, ub_ref, vtb_ref, q_sc):
    i = pl.program_id(0)
    tm = x_ref.shape[0]
    ub_ref[...] = uf_ref[...].astype(BF16)
    vtb_ref[...] = vf_ref[...].T.astype(BF16)
    u = gc_ref[...] * hc_ref[...]
    first = (i % tiles_per_seq) == 0
    u_prev = jnp.where(first, 0.0, gcp_ref[...] * hcp_ref[...])
    u_ext = jnp.concatenate([u_prev, u], axis=0)
    cw = cw_ref[...]
    y = (cw[0:1, :] * u_ext[SUBLANES - 2:SUBLANES - 2 + tm, :]
         + cw[1:2, :] * u_ext[SUBLANES - 1:SUBLANES - 1 + tm, :]
         + cw[2:3, :] * u) + cb_ref[...]
    o_conv = gb_ref[...] * y
    mixed = jnp.concatenate([_rms(oa_ref[...], ag_ref[...]), _rms(o_conv, cg_ref[...])], axis=1)
    x1 = x_ref[...] + jnp.dot(mixed.astype(BF16), wo_ref[...], preferred_element_type=F32)
    x1_ref[...] = x1
    xn = _rms(x1, fg_ref[...]).astype(BF16)
    xnt_ref[...] = xn.T
    q_sc[...] = jnp.dot(xn, wq_ref[...], preferred_element_type=F32)

    def head(h, carry):
        c0 = pl.multiple_of(h * PEER_DK, PEER_DK)
        q1 = q_sc[:, pl.ds(c0, D_HALF)].astype(BF16)
        q2 = q_sc[:, pl.ds(c0 + D_HALF, D_HALF)].astype(BF16)
        nt = (((1,), (1,)), ((), ()))
        s1 = lax.dot_general(keys_ref[0, h], q1, nt, preferred_element_type=F32)
        s2 = lax.dot_general(keys_ref[1, h], q2, nt, preferred_element_type=F32)

        def route(exact):
            v1, i1, _, _, took1 = _extract_topk(s1, PEER_TOPK, False, exact)
            v2, _, rk2, _, took2 = _extract_topk(s2, PEER_TOPK, True, exact)
            widths = [PEER_TOPK // (a + 1) for a in range(PEER_TOPK)]
            pad = -sum(widths) % SUBLANES
            cand = jnp.concatenate(
                [v1[a:a + 1, :] + v2[0:widths[a], :] for a in range(PEER_TOPK)]
                + [jnp.full((pad, tm), -jnp.inf, F32)], axis=0)
            best, _, _, left, took = _extract_topk(cand, PEER_TOPK, False, exact)
            picked = jnp.where(left == cand, 0.0, 1.0)
            z = jnp.sum(jnp.exp(best - best[0:1, :]), axis=0, keepdims=True)
            cnt = jnp.zeros_like(s1)
            key_row = lax.broadcasted_iota(jnp.int32, s1.shape, 0).astype(F32) if exact else None
            off = 0
            for a in range(PEER_TOPK):
                n_a = jnp.sum(picked[off:off + widths[a], :], axis=0, keepdims=True)
                at_a = key_row == i1[a:a + 1, :] if exact else s1 == v1[a:a + 1, :]
                cnt = jnp.where(at_a, n_a, cnt)
                off += widths[a]
            for c in range(tm // LANES):
                cnt_ref[c, h] = cnt[:, c * LANES:(c + 1) * LANES]
            e1 = jnp.exp(s1 - v1[0:1, :])
            for c in range(tm // LANES):
                e1_ref[c, h] = e1[:, c * LANES:(c + 1) * LANES]
            rk2_ref[h] = rk2.astype(BF16)
            e2_ref[h] = (jnp.exp(s2 - v2[0:1, :]) * (0.5 / z)).astype(BF16)
            k = float(PEER_TOPK)
            return jnp.where((took1 == k) & (took2 == k) & (took == k), 0.0, 1.0)

        tied = route(exact=False)

        @pl.when(jnp.max(tied) > 0.0)
        def _():
            route(exact=True)

        return carry

    lax.fori_loop(0, PEER_HEADS, head, 0)


def _mix_route(x2, oa, cv, conv_w, conv_b, attn_g, conv_g, wo_bf16, ffn_g, wq_bf16, keys_bf16,
               u_f32, v_f32, seq):
    m = x2.shape[0]
    tm = TM_MIX
    n_exp = u_f32.shape[0]
    te = n_exp // (m // tm)
    assert te * (m // tm) == n_exp and te % LANES == 0
    row = lambda i: (i, 0)
    const2 = lambda i: (0, 0)
    halo = tm // SUBLANES
    return pl.pallas_call(
        functools.partial(_mix_route_kernel, seq // tm),
        grid=(m // tm,),
        in_specs=[
            pl.BlockSpec((tm, D_MODEL), row),
            pl.BlockSpec((tm, SB_WIDTH), row),
            pl.BlockSpec((tm, CONV_WIDTH), lambda i: (i, 0)),
            pl.BlockSpec((tm, CONV_WIDTH), lambda i: (i, 1)),
            pl.BlockSpec((tm, CONV_WIDTH), lambda i: (i, 2)),
            pl.BlockSpec((SUBLANES, CONV_WIDTH), lambda i: (jnp.maximum(i * halo - 1, 0), 1)),
            pl.BlockSpec((SUBLANES, CONV_WIDTH), lambda i: (jnp.maximum(i * halo - 1, 0), 2)),
            pl.BlockSpec((CONV_K, CONV_WIDTH), const2),
            pl.BlockSpec((1, CONV_WIDTH), const2),
            pl.BlockSpec((1, SB_WIDTH), const2),
            pl.BlockSpec((1, CONV_WIDTH), const2),
            pl.BlockSpec((D_MODEL, D_MODEL), const2),
            pl.BlockSpec((1, D_MODEL), const2),
            pl.BlockSpec((D_MODEL, PEER_HEADS * PEER_DK), const2),
            pl.BlockSpec((2, PEER_HEADS, N_KEYS, D_HALF), lambda i: (0, 0, 0, 0)),
            pl.BlockSpec((te, D_MODEL), row),
            pl.BlockSpec((te, D_MODEL), row),
        ],
        out_specs=[
            pl.BlockSpec((tm, D_MODEL), row),
            pl.BlockSpec((D_MODEL, tm), lambda i: (0, i)),
        ] + [pl.BlockSpec((tm // LANES, PEER_HEADS, N_KEYS, LANES), lambda i: (i, 0, 0, 0))] * 2
        + [pl.BlockSpec((PEER_HEADS, N_KEYS, tm), lambda i: (0, 0, i))] * 2
        + [pl.BlockSpec((te, D_MODEL), row), pl.BlockSpec((D_MODEL, te), lambda i: (0, i))],
        out_shape=[
            jax.ShapeDtypeStruct((m, D_MODEL), F32),
            jax.ShapeDtypeStruct((D_MODEL, m), BF16),
        ] + [jax.ShapeDtypeStruct((m // LANES, PEER_HEADS, N_KEYS, LANES), F32)] * 2
        + [jax.ShapeDtypeStruct((PEER_HEADS, N_KEYS, m), BF16)] * 2
        + [jax.ShapeDtypeStruct((n_exp, D_MODEL), BF16), jax.ShapeDtypeStruct((D_MODEL, n_exp), BF16)],
        scratch_shapes=[pltpu.VMEM((tm, PEER_HEADS * PEER_DK), F32)],
        compiler_params=pltpu.CompilerParams(
            dimension_semantics=("parallel",), vmem_limit_bytes=VMEM_LIMIT),
        name="mix_route",
    )(x2, oa, cv, cv, cv, cv, cv, conv_w, conv_b.reshape(1, -1), attn_g.reshape(1, -1),
      conv_g.reshape(1, -1), wo_bf16, ffn_g.reshape(1, -1), wq_bf16, keys_bf16, u_f32, v_f32)


def _peer_kernel(xnt_ref, u_ref, vt_ref, cnt_ref, e1_ref, rk2_ref, e2_ref, x1_ref, fg_ref,
                 out_ref, acc_sc):
    e = pl.program_id(1)

    @pl.when(e == 0)
    def _():
        acc_sc[...] = jnp.zeros_like(acc_sc)

    xnt = xnt_ref[...]
    rows = TE_PEER // N_KEYS
    w_parts = []
    for r in range(rows):
        sl = slice(r * N_KEYS, (r + 1) * N_KEYS)
        act = jnp.dot(u_ref[sl, :], xnt, preferred_element_type=F32)
        tiles = (N_KEYS // BF16_ROWS, BF16_ROWS, act.shape[1])
        gate = jnp.zeros(tiles, BF16)
        for h in range(PEER_HEADS):
            lane_tiles = range(act.shape[1] // LANES)
            row = pl.ds(r, BF16_ROWS, stride=0)
            cnt_t = jnp.concatenate([cnt_ref[c, h, row, :] for c in lane_tiles], axis=1).astype(BF16)
            e1_t = jnp.concatenate([e1_ref[c, h, row, :] for c in lane_tiles], axis=1).astype(BF16)
            picked = rk2_ref[h].reshape(tiles) < cnt_t[None]
            gate = gate + jnp.where(picked, e2_ref[h].reshape(tiles) * e1_t[None], 0.0)
        a_b = act.astype(BF16)
        inner = a_b * (GELU_C0 + (GELU_C0 * GELU_C1) * (a_b * a_b))
        w_parts.append(a_b * (1.0 + jnp.tanh(inner)) * gate.reshape(act.shape))
    w = jnp.concatenate(w_parts, axis=0)
    acc_sc[...] += jnp.dot(vt_ref[...], w, preferred_element_type=F32)

    @pl.when(e == pl.num_programs(1) - 1)
    def _():
        out_ref[...] = _rms(x1_ref[...] + acc_sc[...].T, fg_ref[...])


def _peer_experts(xnt, u_bf16, vt_bf16, cnt, e1, rk2, e2, x1, final_g):
    m = xnt.shape[1]
    n_exp = u_bf16.shape[0]
    tm, te = TM_PEER, TE_PEER
    rows = te // N_KEYS
    per_row = pl.BlockSpec((tm // LANES, PEER_HEADS, rows, LANES), lambda t, e: (t, 0, e, 0))
    per_key = pl.BlockSpec((PEER_HEADS, N_KEYS, tm), lambda t, e: (0, 0, t))
    return pl.pallas_call(
        _peer_kernel,
        grid=(m // tm, n_exp // te),
        in_specs=[
            pl.BlockSpec((D_MODEL, tm), lambda t, e: (0, t)),
            pl.BlockSpec((te, D_MODEL), lambda t, e: (e, 0)),
            pl.BlockSpec((D_MODEL, te), lambda t, e: (0, e)),
            per_row, per_row, per_key, per_key,
            pl.BlockSpec((tm, D_MODEL), lambda t, e: (t, 0)),
            pl.BlockSpec((1, D_MODEL), lambda t, e: (0, 0)),
        ],
        out_specs=pl.BlockSpec((tm, D_MODEL), lambda t, e: (t, 0)),
        out_shape=jax.ShapeDtypeStruct((m, D_MODEL), F32),
        scratch_shapes=[pltpu.VMEM((D_MODEL, tm), F32)],
        compiler_params=pltpu.CompilerParams(
            dimension_semantics=("parallel", "arbitrary"), vmem_limit_bytes=VMEM_LIMIT),
        name="peer_experts",
    )(xnt, u_bf16, vt_bf16, cnt, e1, rk2, e2, x1, final_g.reshape(1, D_MODEL))


def kernel(x, mix_norm_g, w_in, conv_w, conv_b, attn_out_g, conv_out_g, w_out, ffn_norm_g,
           peer_wq, peer_keys, peer_u, peer_v, final_norm_g):
    b, s, d = x.shape
    assert w_in.shape[0] == 1, "single-layer block"
    w_in_b = w_in[0].astype(BF16)
    w_out_b = w_out[0].astype(BF16)
    wq_b = peer_wq[0].astype(BF16)
    keys_b = peer_keys[0].astype(BF16)
    m = b * s
    x2 = x.reshape(m, d)
    qkv, cv = _inproj(x2, mix_norm_g[0], w_in_b)
    oa = _attention(qkv.reshape(b, s, 3 * SB_WIDTH)).reshape(m, SB_WIDTH)
    x1, xnt, cnt, e1, rk2, e2, u_b, vt_b = _mix_route(
        x2, oa, cv, conv_w[0], conv_b[0], attn_out_g[0], conv_out_g[0],
        w_out_b, ffn_norm_g[0], wq_b, keys_b, peer_u[0], peer_v[0], s)
    out = _peer_experts(xnt, u_b, vt_b, cnt, e1, rk2, e2, x1, final_norm_g)
    return out.reshape(b, s, d)
```
